```python
import jax, jax.numpy as jnp
from jax import lax
import numpy as np

D_MODEL = 2048
BATCH = 1
SEQ = 8192
DEPTH = 4
DEC_BATCH = 8
DEC_SEQ = 64
PAST_LEN = 4096

CHUNK = 64
N_A_LAYERS = DEPTH // 2
N_B_LAYERS = DEPTH - N_A_LAYERS
CONV_W = 3
N_HEADS = 16
HEAD_DIM = D_MODEL // N_HEADS
N_BACK_CHUNKS = 8
BAND = (N_BACK_CHUNKS + 1) * CHUNK
BAND_PAST = N_BACK_CHUNKS * CHUNK
REL_CLIP = 128
N_REL = 2 * REL_CLIP + 1
N_GROUPS = 4
EXPERTS_PER_GROUP = 4
N_EXPERTS = N_GROUPS * EXPERTS_PER_GROUP
TOP_K_INNER = 2
D_EXPERT = 512
EPS = 1e-6

kernel_name = "yoco_shortconv_chunkband_hmoe_step"


def rmsnorm(x, g):
    xf = x.astype(jnp.float32)
    y = xf * lax.rsqrt(jnp.mean(xf * xf, axis=-1, keepdims=True) + EPS)
    return (y * g.astype(jnp.float32)).astype(x.dtype)


def short_conv_mixer(x, conv_prev, w_in, conv_w, w_out):
    T = x.shape[1]
    b_gate, c_gate, h = jnp.split(x @ w_in, 3, axis=-1)
    u = c_gate * h
    upad = jnp.concatenate([conv_prev.astype(u.dtype), u], axis=1)
    conv = sum(conv_w[k] * upad[:, k:k + T] for k in range(CONV_W))
    y = (b_gate * conv) @ w_out
    return y, upad[:, -(CONV_W - 1):]


def rel_bias(table, dist):
    idx = jnp.clip(dist, -REL_CLIP, REL_CLIP) + REL_CLIP
    return table[:, idx].astype(jnp.float32)


def shared_kv(x, g_kv, w_kv):
    B, T, _ = x.shape
    k, v = jnp.split(rmsnorm(x, g_kv) @ w_kv, 2, axis=-1)
    return (k.reshape(B, T, N_HEADS, HEAD_DIM), v.reshape(B, T, N_HEADS, HEAD_DIM))


def gather_band(k):
    B, T = k.shape[:2]
    nc = T // CHUNK
    kc = k.reshape(B, nc, CHUNK, N_HEADS, HEAD_DIM)
    kp = jnp.pad(kc, ((0, 0), (N_BACK_CHUNKS, 0), (0, 0), (0, 0), (0, 0)))
    return jnp.concatenate([kp[:, j:j + nc] for j in range(N_BACK_CHUNKS + 1)], axis=2)


def chunk_band_attn_prompt(x, k_band, v_band, w_q, w_o, table):
    B, T, _ = x.shape
    nc = T // CHUNK
    q = (x @ w_q).reshape(B, nc, CHUNK, N_HEADS, HEAD_DIM)
    s = jnp.einsum('bcqhd,bckhd->bchqk', q, k_band).astype(jnp.float32) * (HEAD_DIM ** -0.5)
    t = jnp.arange(CHUNK)
    m = jnp.arange(BAND)
    bias = rel_bias(table, BAND_PAST + t[:, None] - m[None, :])
    key_pos = (jnp.arange(nc)[:, None] - N_BACK_CHUNKS) * CHUNK + m[None, :]
    s = jnp.where((key_pos >= 0)[None, :, None, None, :], s + bias[None, None], -jnp.inf)
    p = jax.nn.softmax(s, axis=-1).astype(x.dtype)
    o = jnp.einsum('bchqk,bckhd->bcqhd', p, v_band).reshape(B, T, D_MODEL)
    return o @ w_o


def chunk_band_attn_step(x, k_all, v_all, n_cache, w_q, w_o, table):
    B, S, _ = x.shape
    q = (x @ w_q).reshape(B, S, N_HEADS, HEAD_DIM)
    s = jnp.einsum('bqhd,bkhd->bhqk', q, k_all).astype(jnp.float32) * (HEAD_DIM ** -0.5)
    key_off = jnp.concatenate([jnp.arange(n_cache) - n_cache, jnp.arange(S)])
    s = s + rel_bias(table, jnp.arange(S)[:, None] - key_off[None, :])[None]
    p = jax.nn.softmax(s, axis=-1).astype(x.dtype)
    o = jnp.einsum('bhqk,bkhd->bqhd', p, v_all).reshape(B, S, D_MODEL)
    return o @ w_o


def hier_moe(x, w_grp, b_grp, w_rt, b_rt, w_gate, w_up, w_down):
    glog = (x @ w_grp + b_grp).astype(jnp.float32)
    gsel = jax.nn.one_hot(jnp.argmax(glog, axis=-1), N_GROUPS, dtype=jnp.float32)
    gp = jnp.sum(jax.nn.softmax(glog, axis=-1) * gsel, axis=-1, keepdims=True)
    elog = (x @ w_rt + b_rt).astype(jnp.float32)
    elog = elog.reshape(elog.shape[:-1] + (N_GROUPS, EXPERTS_PER_GROUP))
    esub = jnp.einsum('btg,btge->bte', gsel, elog)
    top_v, top_i = lax.top_k(esub, TOP_K_INNER)
    gates = jax.nn.softmax(top_v, axis=-1) * gp
    eid = jnp.argmax(gsel, axis=-1)[..., None] * EXPERTS_PER_GROUP + top_i
    combine = jnp.sum(jax.nn.one_hot(eid, N_EXPERTS, dtype=jnp.float32) * gates[..., None], axis=-2)
    hg = jnp.einsum('btd,edf->btef', x, w_gate)
    hu = jnp.einsum('btd,edf->btef', x, w_up)
    h = jax.nn.silu(hg) * hu * combine[..., None].astype(x.dtype)
    return jnp.einsum('btef,efd->btd', h, w_down)


def setup_inputs(seed: int = 0) -> dict:
    key = jax.random.key(seed)
    ks = jax.random.split(key, 24)
    f32 = jnp.float32
    nrm = lambda k, shape, scale: jax.random.normal(k, shape, f32) * scale
    kv_buf = min(BAND_PAST, PAST_LEN)
    D, F = D_MODEL, D_EXPERT
    return {
        "x_prompt": nrm(ks[0], (BATCH, SEQ, D), 1.0),
        "x_sample": nrm(ks[1], (DEC_BATCH, DEC_SEQ, D), 1.0),
        "state_conv": nrm(ks[2], (N_A_LAYERS, DEC_BATCH, CONV_W - 1, D), 1.0),
        "cache_k": nrm(ks[3], (DEC_BATCH, kv_buf, N_HEADS, HEAD_DIM), 1.0),
        "cache_v": nrm(ks[4], (DEC_BATCH, kv_buf, N_HEADS, HEAD_DIM), 1.0),
        "g_mix": 1.0 + nrm(ks[5], (DEPTH, D), 0.02),
        "g_ffn": 1.0 + nrm(ks[6], (DEPTH, D), 0.02),
        "w_in_a": nrm(ks[7], (N_A_LAYERS, D, 3 * D), D ** -0.5),
        "conv_a": nrm(ks[8], (N_A_LAYERS, CONV_W, D), 0.5),
        "w_out_a": nrm(ks[9], (N_A_LAYERS, D, D), D ** -0.5),
        "g_kv": 1.0 + nrm(ks[10], (D,), 0.02),
        "w_kv": nrm(ks[11], (D, 2 * D), D ** -0.5),
        "w_q_b": nrm(ks[12], (N_B_LAYERS, D, D), D ** -0.5),
        "w_o_b": nrm(ks[13], (N_B_LAYERS, D, D), D ** -0.5),
        "rel_bias_b": nrm(ks[14], (N_B_LAYERS, N_HEADS, N_REL), 0.1),
        "w_group": nrm(ks[15], (DEPTH, D, N_GROUPS), D ** -0.5),
        "b_group": nrm(ks[16], (DEPTH, N_GROUPS), 0.01),
        "w_router": nrm(ks[17], (DEPTH, D, N_EXPERTS), D ** -0.5),
        "b_router": nrm(ks[18], (DEPTH, N_EXPERTS), 0.01),
        "w_gate": nrm(ks[19], (DEPTH, N_EXPERTS, D, F), D ** -0.5),
        "w_up": nrm(ks[20], (DEPTH, N_EXPERTS, D, F), D ** -0.5),
        "w_down": nrm(ks[21], (DEPTH, N_EXPERTS, F, D), F ** -0.5),
        "g_final": 1.0 + nrm(ks[22], (D,), 0.02),
    }


def reference(x_prompt, x_sample, state_conv, cache_k, cache_v, g_mix, g_ffn, w_in_a, conv_a,
              w_out_a, g_kv, w_kv, w_q_b, w_o_b, rel_bias_b, w_group, b_group, w_router,
              b_router, w_gate, w_up, w_down, g_final):
    xp, xs = x_prompt, x_sample
    Bp, Tp = xp.shape[0], xp.shape[1]

    def channel(x, l):
        return x + hier_moe(rmsnorm(x, g_ffn[l]), w_group[l], b_group[l], w_router[l],
                            b_router[l], w_gate[l], w_up[l], w_down[l])

    conv_p, conv_s = [], []
    for l in range(N_A_LAYERS):
        zeros = jnp.zeros((Bp, CONV_W - 1, D_MODEL), xp.dtype)
        yp, cp = short_conv_mixer(rmsnorm(xp, g_mix[l]), zeros, w_in_a[l], conv_a[l], w_out_a[l])
        ys, cs = short_conv_mixer(rmsnorm(xs, g_mix[l]), state_conv[l], w_in_a[l], conv_a[l], w_out_a[l])
        xp = channel(xp + yp, l)
        xs = channel(xs + ys, l)
        conv_p.append(cp)
        conv_s.append(cs)
    new_conv_prompt = jnp.stack(conv_p, axis=0)
    new_conv_sample = jnp.stack(conv_s, axis=0)

    kp, vp = shared_kv(xp, g_kv, w_kv)
    k_band, v_band = gather_band(kp), gather_band(vp)
    ks_new, vs_new = shared_kv(xs, g_kv, w_kv)
    n_cache = cache_k.shape[1]
    k_all = jnp.concatenate([cache_k, ks_new], axis=1)
    v_all = jnp.concatenate([cache_v, vs_new], axis=1)

    for j in range(N_B_LAYERS):
        l = N_A_LAYERS + j
        yp = chunk_band_attn_prompt(rmsnorm(xp, g_mix[l]), k_band, v_band, w_q_b[j], w_o_b[j], rel_bias_b[j])
        ys = chunk_band_attn_step(rmsnorm(xs, g_mix[l]), k_all, v_all, n_cache, w_q_b[j], w_o_b[j], rel_bias_b[j])
        xp = channel(xp + yp, l)
        xs = channel(xs + ys, l)

    y_prompt = rmsnorm(xp, g_final)
    y_sample = rmsnorm(xs, g_final)
    keep_p = min(BAND_PAST, Tp)
    new_k_prompt = kp[:, Tp - keep_p:]
    new_v_prompt = vp[:, Tp - keep_p:]
    new_k_sample = k_all[:, k_all.shape[1] - n_cache:]
    new_v_sample = v_all[:, v_all.shape[1] - n_cache:]
    return (y_prompt, y_sample, new_conv_prompt, new_k_prompt, new_v_prompt,
            new_conv_sample, new_k_sample, new_v_sample)
```

```python
import functools

import jax
import jax.numpy as jnp
from jax import lax
from jax.experimental import pallas as pl
from jax.experimental.pallas import tpu as pltpu

F32 = jnp.float32
BF16 = jnp.bfloat16

CHUNK = 64
N_BACK_CHUNKS = 8
REL_CLIP = 128
N_GROUPS = 4
EXPERTS_PER_GROUP = 4
N_EXPERTS = N_GROUPS * EXPERTS_PER_GROUP
EPS = 1e-6

LANES = 128
EXPERT_TILE = 256
ATTN_GROUP = 4
VMEM_LIMIT = 56 * 1024 * 1024


def _params(sem, **kw):
    return pltpu.CompilerParams(dimension_semantics=sem, vmem_limit_bytes=VMEM_LIMIT, **kw)


def _rms(x):
    return x * lax.rsqrt(jnp.mean(x * x, axis=-1, keepdims=True) + EPS)


def _norm_kernel(x_ref, g_ref, o_ref):
    o_ref[...] = (_rms(x_ref[...]) * g_ref[...]).astype(o_ref.dtype)


def rmsnorm_call(x, g, dtype, tm=512):
    M, D = x.shape
    return pl.pallas_call(
        _norm_kernel,
        grid=(M // tm,),
        in_specs=[pl.BlockSpec((tm, D), lambda i: (i, 0)), pl.BlockSpec((1, D), lambda i: (0, 0))],
        out_specs=pl.BlockSpec((tm, D), lambda i: (i, 0)),
        out_shape=jax.ShapeDtypeStruct((M, D), dtype),
        compiler_params=_params(("arbitrary",)),
        name="rmsnorm",
    )(x, g.reshape(1, D))


def _conv_in_kernel(xn_ref, wb_ref, wc_ref, wh_ref, cw_ref, inj1_ref, inj2_ref,
                    g_ref, cp_ref, cs_ref, wbs, wcs, whs, carry, u_s, *, n_prompt_tiles, n_seq, seq_len):
    i = pl.program_id(1)
    tm = u_s.shape[0]

    @pl.when(i == 0)
    def _():
        wbs[...] = wb_ref[...].astype(BF16)
        wcs[...] = wc_ref[...].astype(BF16)
        whs[...] = wh_ref[...].astype(BF16)
        carry[...] = jnp.zeros_like(carry)

    xn = xn_ref[...]
    b = jnp.dot(xn, wbs[...], preferred_element_type=F32)
    c = jnp.dot(xn, wcs[...], preferred_element_type=F32)
    h = jnp.dot(xn, whs[...], preferred_element_type=F32)
    u = c * h
    u_s[...] = u
    is_sample = i >= n_prompt_tiles
    row = lax.broadcasted_iota(jnp.int32, u.shape, 0)
    rowm = row & jnp.where(is_sample, seq_len - 1, 0x7FFFFFFF)
    c0 = carry[0:1, :]
    c1 = carry[1:2, :]
    e1 = jnp.where(is_sample, inj1_ref[...], c1)
    e2 = jnp.where(is_sample, inj2_ref[...], jnp.where(row == 0, c0, c1))
    up1 = jnp.where(rowm == 0, e1, pltpu.roll(u, 1, 0))
    up2 = jnp.where(rowm < 2, e2, pltpu.roll(u, 2, 0))
    cw = cw_ref[...]
    conv = cw[0:1, :] * up2 + cw[1:2, :] * up1 + cw[2:3, :] * u
    g_ref[...] = (b * conv).astype(g_ref.dtype)
    carry[...] = u_s[tm - 2:tm, :]

    @pl.when(i == n_prompt_tiles - 1)
    def _():
        cp_ref[...] = u_s[tm - 2:tm, :]

    @pl.when(i == n_prompt_tiles)
    def _():
        for s in range(n_seq):
            cs_ref[s] = u_s[(s + 1) * seq_len - 2:(s + 1) * seq_len, :]


def conv_in_call(xn, w_in_all, conv_all, layer, inj1, inj2, n_prompt_rows, n_seq, seq_len, tn=512):
    M, D = xn.shape
    tm = n_seq * seq_len
    assert M == n_prompt_rows + tm and n_prompt_rows % tm == 0 and D % tn == 0
    nj, ni = D // tn, M // tm
    kern = functools.partial(_conv_in_kernel, n_prompt_tiles=n_prompt_rows // tm, n_seq=n_seq, seq_len=seq_len)
    wspec = lambda k: pl.BlockSpec((None, D, tn), lambda j, i, k=k: (layer, 0, j + k * nj))
    return pl.pallas_call(
        kern,
        grid=(nj, ni),
        in_specs=[
            pl.BlockSpec((tm, D), lambda j, i: (i, 0)),
            wspec(0), wspec(1), wspec(2),
            pl.BlockSpec((None, 3, tn), lambda j, i: (layer, 0, j)),
            pl.BlockSpec((tm, tn), lambda j, i: (0, j)),
            pl.BlockSpec((tm, tn), lambda j, i: (0, j)),
        ],
        out_specs=[
            pl.BlockSpec((tm, tn), lambda j, i: (i, j)),
            pl.BlockSpec((2, tn), lambda j, i: (0, j)),
            pl.BlockSpec((n_seq, 2, tn), lambda j, i: (0, 0, j)),
        ],
        out_shape=[
            jax.ShapeDtypeStruct((M, D), BF16),
            jax.ShapeDtypeStruct((2, D), F32),
            jax.ShapeDtypeStruct((n_seq, 2, D), F32),
        ],
        scratch_shapes=[pltpu.VMEM((D, tn), BF16)] * 3 + [pltpu.VMEM((2, tn), F32), pltpu.VMEM((tm, tn), F32)],
        compiler_params=_params(("arbitrary", "arbitrary")),
        name="conv_in",
    )(xn, w_in_all, w_in_all, w_in_all, conv_all, inj1, inj2)


def _matmul_kernel(x_ref, w_ref, *rest):
    out_refs, ws = rest[:-1], rest[-1]

    @pl.when(pl.program_id(1) == 0)
    def _():
        ws[...] = w_ref[...].astype(BF16)

    y = jnp.dot(x_ref[...], ws[...], preferred_element_type=F32)
    for o in out_refs:
        o[...] = y.astype(o.dtype)


def matmul_call(x, w, w_index, out_dtypes, tm=512, tn=512, name="matmul"):
    M, K = x.shape
    N = w.shape[-1]
    lead = tuple(w_index)
    wblock = (None,) * len(lead) + (K, tn)
    return pl.pallas_call(
        _matmul_kernel,
        grid=(N // tn, M // tm),
        in_specs=[pl.BlockSpec((tm, K), lambda j, i: (i, 0)),
                  pl.BlockSpec(wblock, lambda j, i: lead + (0, j))],
        out_specs=[pl.BlockSpec((tm, tn), lambda j, i: (i, j)) for _ in out_dtypes],
        out_shape=[jax.ShapeDtypeStruct((M, N), dt) for dt in out_dtypes],
        scratch_shapes=[pltpu.VMEM((K, tn), BF16)],
        compiler_params=_params(("arbitrary", "arbitrary")),
        name=name,
    )(x, w)


def _route(logits, carry):
    tm = logits.shape[0]
    lane = lax.broadcasted_iota(jnp.int32, logits.shape, 1).astype(F32)
    neg = -jnp.inf
    big = float(LANES)
    gl = jnp.where(lane < N_GROUPS, logits, neg)
    gmax = jnp.max(gl, axis=-1, keepdims=True)
    gidx = jnp.min(jnp.where(gl == gmax, lane, big), axis=-1, keepdims=True)
    gp = 1.0 / jnp.sum(jnp.exp(gl - gmax), axis=-1, keepdims=True)
    lo = N_GROUPS + gidx * EXPERTS_PER_GROUP
    el = jnp.where(lane >= lo, jnp.where(lane < lo + EXPERTS_PER_GROUP, logits, neg), neg)
    t1 = jnp.max(el, axis=-1, keepdims=True)
    i1 = jnp.min(jnp.where(el == t1, lane, big), axis=-1, keepdims=True)
    el2 = jnp.where(lane == i1, neg, el)
    t2 = jnp.max(el2, axis=-1, keepdims=True)
    i2 = jnp.min(jnp.where(el2 == t2, lane, big), axis=-1, keepdims=True)
    ex = jnp.exp(t2 - t1)
    den = 1.0 + ex
    w0 = (1.0 / den) * gp
    w1 = (ex / den) * gp
    hit1 = lane == i1
    hit2 = lane == i2
    cnt = jnp.where(hit1, 1.0, jnp.where(hit2, 1.0, 0.0))
    r_i = lax.broadcasted_iota(jnp.int32, (tm, tm), 0)
    c_i = lax.broadcasted_iota(jnp.int32, (tm, tm), 1)
    tri = jnp.where(c_i < r_i, 1.0, 0.0).astype(BF16)
    before = jnp.dot(tri, cnt.astype(BF16), preferred_element_type=F32) + carry[...]
    rank0 = jnp.sum(jnp.where(hit1, before, 0.0), axis=-1, keepdims=True)
    rank1 = jnp.sum(jnp.where(hit2, before, 0.0), axis=-1, keepdims=True)
    carry[...] = carry[...] + jnp.sum(cnt, axis=0, keepdims=True)
    vals = (i1 - N_GROUPS, i2 - N_GROUPS, rank0, rank1, w0, w1)
    route = jnp.zeros_like(logits)
    for k, v in enumerate(vals):
        route = jnp.where(lane == float(k), v, route)
    return route


def _proj_route_kernel(a_ref, w_ref, x_ref, gf_ref, wr_ref, br_ref,
                       x1_ref, xn_ref, route_ref, cnt_ref, ws, carry):
    K = ws.shape[0]

    @pl.when(pl.program_id(0) == 0)
    def _():
        step = 256

        def body(k, _):
            r = pl.multiple_of(k * step, step)
            ws[pl.ds(r, step), :] = w_ref[pl.ds(r, step), :].astype(BF16)
            return 0

        lax.fori_loop(0, K // step, body, 0)
        carry[...] = jnp.zeros_like(carry)

    y = jnp.dot(a_ref[...], ws[...], preferred_element_type=F32)
    x1 = x_ref[...] + y
    x1_ref[...] = x1
    xn = _rms(x1) * gf_ref[...]
    xn_ref[...] = xn
    logits = jnp.dot(xn, wr_ref[...], precision=lax.Precision.HIGHEST, preferred_element_type=F32) + br_ref[...]
    route_ref[...] = _route(logits, carry)
    cnt_ref[...] = carry[...]


def proj_route_call(a, w_all, w_index, x, g_ffn, w_route, b_route, tm=256):
    M, K = a.shape
    D = x.shape[1]
    lead = tuple(w_index)
    wblock = (None,) * len(lead) + (K, D)
    const = lambda i: (0, 0)
    rowblk = lambda i: (i, 0)
    return pl.pallas_call(
        _proj_route_kernel,
        grid=(M // tm,),
        in_specs=[
            pl.BlockSpec((tm, K), rowblk),
            pl.BlockSpec(wblock, lambda i: lead + (0, 0), pipeline_mode=pl.Buffered(1)),
            pl.BlockSpec((tm, D), rowblk),
            pl.BlockSpec((1, D), const),
            pl.BlockSpec((D, LANES), const),
            pl.BlockSpec((1, LANES), const),
        ],
        out_specs=[
            pl.BlockSpec((tm, D), rowblk),
            pl.BlockSpec((tm, D), rowblk),
            pl.BlockSpec((tm, LANES), rowblk),
            pl.BlockSpec((1, LANES), const),
        ],
        out_shape=[
            jax.ShapeDtypeStruct((M, D), F32),
            jax.ShapeDtypeStruct((M, D), F32),
            jax.ShapeDtypeStruct((M, LANES), F32),
            jax.ShapeDtypeStruct((1, LANES), F32),
        ],
        scratch_shapes=[pltpu.VMEM((K, D), BF16), pltpu.VMEM((1, LANES), F32)],
        compiler_params=_params(("arbitrary",)),
        name="proj_route",
    )(a, w_all, x, g_ffn.reshape(1, D), w_route, b_route)


def routing_plan(route, counts, n_tiles_max):
    te = EXPERT_TILE
    eid = route[:, 0:2].astype(jnp.int32)
    rank = route[:, 2:4].astype(jnp.int32)
    cnt = counts[0, N_GROUPS:N_GROUPS + N_EXPERTS].astype(jnp.int32)
    tiles = (cnt + te - 1) // te
    tile_end = jnp.cumsum(tiles)
    offs = (tile_end - tiles) * te
    onehot = eid[..., None] == jnp.arange(N_EXPERTS, dtype=jnp.int32)
    dest = jnp.sum(jnp.where(onehot, offs, 0), axis=-1) + rank
    n_tiles = tile_end[-1]
    tid = jnp.arange(n_tiles_max, dtype=jnp.int32)
    tile_expert = jnp.sum((jnp.minimum(tid, n_tiles - 1)[:, None] >= tile_end[None, :]).astype(jnp.int32), axis=1)
    pad_start = offs + cnt
    pad_len = tiles * te - cnt
    return dest.reshape(-1), tile_expert, n_tiles.reshape(1), pad_start, pad_len


def _dispatch_kernel(dest_ref, pstart_ref, plen_ref, nt_ref, xn_hbm, xs_hbm, zbuf, sem, zsem, tsem, *, tm):
    i = pl.program_id(0)
    base = i * tm
    te = zbuf.shape[0]
    n_tiles_max = xs_hbm.shape[0] // te

    def row_copy(t, slot):
        return pltpu.make_async_copy(xn_hbm.at[pl.ds(t, 1), :], xs_hbm.at[pl.ds(slot, 1), :], sem)

    def zero_copy(slot):
        return pltpu.make_async_copy(zbuf.at[pl.ds(0, 1), :], xs_hbm.at[pl.ds(slot, 1), :], zsem)

    def zero_tile_copy(tile):
        return pltpu.make_async_copy(zbuf, xs_hbm.at[pl.ds(pl.multiple_of(tile * te, te), te), :], tsem)

    def issue(r, _):
        t = base + r
        row_copy(t, dest_ref[2 * t]).start()
        row_copy(t, dest_ref[2 * t + 1]).start()
        return 0

    lax.fori_loop(0, tm, issue, 0)

    @pl.when(i == 0)
    def _():
        zbuf[...] = jnp.zeros_like(zbuf)
        for e in range(N_EXPERTS):
            def zissue(r, _, e=e):
                zero_copy(pstart_ref[e] + r).start()
                return 0

            lax.fori_loop(0, plen_ref[e], zissue, 0)

        def tissue(t, _):
            zero_tile_copy(t).start()
            return 0

        lax.fori_loop(nt_ref[0], n_tiles_max, tissue, 0)
        for e in range(N_EXPERTS):
            def zwait(r, _):
                zero_copy(0).wait()
                return 0

            lax.fori_loop(0, plen_ref[e], zwait, 0)

        def twait(t, _):
            zero_tile_copy(0).wait()
            return 0

        lax.fori_loop(nt_ref[0], n_tiles_max, twait, 0)

    def wait(r, _):
        row_copy(0, 0).wait()
        row_copy(0, 0).wait()
        return 0

    lax.fori_loop(0, tm, wait, 0)


def dispatch_call(xn, dest, pad_start, pad_len, n_tiles, n_slots, tm=512):
    M, D = xn.shape
    kern = functools.partial(_dispatch_kernel, tm=tm)
    return pl.pallas_call(
        kern,
        grid_spec=pltpu.PrefetchScalarGridSpec(
            num_scalar_prefetch=4,
            grid=(M // tm,),
            in_specs=[pl.BlockSpec(memory_space=pl.ANY)],
            out_specs=pl.BlockSpec(memory_space=pl.ANY),
            scratch_shapes=[pltpu.VMEM((EXPERT_TILE, D), xn.dtype)] + [pltpu.SemaphoreType.DMA(())] * 3,
        ),
        out_shape=jax.ShapeDtypeStruct((n_slots, D), xn.dtype),
        compiler_params=_params(("arbitrary",)),
        name="dispatch",
    )(dest, pad_start, pad_len, n_tiles, xn)


def _experts_kernel(te_ref, nt_ref, xs_ref, wg_ref, wu_ref, wd_ref, ys_ref, wgs, wus, wds):
    t = pl.program_id(0)
    valid = t < nt_ref[0]
    prev = te_ref[jnp.maximum(t - 1, 0)]
    fresh = jnp.logical_or(t == 0, te_ref[t] != prev)

    @pl.when(jnp.logical_and(valid, fresh))
    def _():
        wgs[...] = wg_ref[...].astype(BF16)
        wus[...] = wu_ref[...].astype(BF16)
        wds[...] = wd_ref[...].astype(BF16)

    @pl.when(valid)
    def _():
        x = xs_ref[...].astype(BF16)
        hg = jnp.dot(x, wgs[...], preferred_element_type=F32)
        hu = jnp.dot(x, wus[...], preferred_element_type=F32)
        h = (hg * (1.0 / (1.0 + jnp.exp(-hg)))) * hu
        ys_ref[...] = jnp.dot(h.astype(BF16), wds[...], preferred_element_type=F32).astype(ys_ref.dtype)

    @pl.when(jnp.logical_not(valid))
    def _():
        ys_ref[...] = jnp.zeros_like(ys_ref)


def experts_call(xs, tile_expert, n_tiles, w_gate, w_up, w_down, layer):
    S, D = xs.shape
    Fd = w_gate.shape[-1]
    te = EXPERT_TILE
    row = lambda t, te_ref, nt_ref: (jnp.minimum(t, nt_ref[0] - 1), 0)
    wsel = lambda t, te_ref, nt_ref: (layer, te_ref[t], 0, 0)
    return pl.pallas_call(
        _experts_kernel,
        grid_spec=pltpu.PrefetchScalarGridSpec(
            num_scalar_prefetch=2,
            grid=(S // te,),
            in_specs=[
                pl.BlockSpec((te, D), row),
                pl.BlockSpec((None, None, D, Fd), wsel),
                pl.BlockSpec((None, None, D, Fd), wsel),
                pl.BlockSpec((None, None, Fd, D), wsel),
            ],
            out_specs=pl.BlockSpec((te, D), lambda t, te_ref, nt_ref: (t, 0)),
            scratch_shapes=[pltpu.VMEM((D, Fd), BF16), pltpu.VMEM((D, Fd), BF16), pltpu.VMEM((Fd, D), BF16)],
        ),
        out_shape=jax.ShapeDtypeStruct((S, D), F32),
        compiler_params=_params(("arbitrary",)),
        name="experts",
    )(tile_expert, n_tiles, xs, w_gate, w_up, w_down)


def _combine_kernel(dest_ref, x1_ref, route_ref, ys_hbm, *rest, tm, n_gain, emit_x):
    gain_refs = rest[:n_gain]
    n_out = n_gain + (1 if emit_x else 0)
    out_refs = rest[n_gain:n_gain + n_out]
    buf0, buf1, sem = rest[n_gain + n_out:]
    base = pl.program_id(0) * tm

    def row_copy(slot, buf, r):
        return pltpu.make_async_copy(ys_hbm.at[pl.ds(slot, 1), :], buf.at[pl.ds(r, 1), :], sem)

    def issue(r, _):
        t = base + r
        row_copy(dest_ref[2 * t], buf0, r).start()
        row_copy(dest_ref[2 * t + 1], buf1, r).start()
        return 0

    lax.fori_loop(0, tm, issue, 0)

    def wait(r, _):
        row_copy(0, buf0, r).wait()
        row_copy(0, buf1, r).wait()
        return 0

    lax.fori_loop(0, tm, wait, 0)

    route = route_ref[...]
    x2 = x1_ref[...] + (route[:, 4:5] * buf0[...] + route[:, 5:6] * buf1[...])
    k = 0
    if emit_x:
        out_refs[0][...] = x2
        k = 1
    if n_gain:
        xh = _rms(x2)
        for g_ref, o_ref in zip(gain_refs, out_refs[k:]):
            o_ref[...] = (xh * g_ref[...]).astype(o_ref.dtype)


def combine_call(x1, route, ys, dest, gains, gain_dtypes, emit_x, tm=256):
    M, D = x1.shape
    n_gain = len(gains)
    kern = functools.partial(_combine_kernel, tm=tm, n_gain=n_gain, emit_x=emit_x)
    rowblk = lambda i, d: (i, 0)
    const = lambda i, d: (0, 0)
    out_dtypes = ([F32] if emit_x else []) + list(gain_dtypes)
    return pl.pallas_call(
        kern,
        grid_spec=pltpu.PrefetchScalarGridSpec(
            num_scalar_prefetch=1,
            grid=(M // tm,),
            in_specs=[pl.BlockSpec((tm, D), rowblk), pl.BlockSpec((tm, LANES), rowblk),
                      pl.BlockSpec(memory_space=pl.ANY)] + [pl.BlockSpec((1, D), const)] * n_gain,
            out_specs=[pl.BlockSpec((tm, D), rowblk) for _ in out_dtypes],
            scratch_shapes=[pltpu.VMEM((tm, D), ys.dtype), pltpu.VMEM((tm, D), ys.dtype),
                            pltpu.SemaphoreType.DMA(())],
        ),
        out_shape=[jax.ShapeDtypeStruct((M, D), dt) for dt in out_dtypes],
        compiler_params=_params(("arbitrary",)),
        name="combine",
    )(dest, x1, route, ys, *[g.reshape(1, D) for g in gains])


def _attn_kernel(q_ref, *rest, n_parts, n_heads, head_dim, mask_parts):
    k_refs = rest[:n_parts]
    v_refs = rest[n_parts:2 * n_parts]
    bias_ref, o_ref = rest[2 * n_parts:]
    blk = pl.program_id(0)
    pw = k_refs[0].shape[0]
    scale = head_dim ** -0.5
    for h in range(n_heads):
        hs = slice(h * head_dim, (h + 1) * head_dim)
        qh = q_ref[:, hs]
        s_parts = []
        for p in range(n_parts):
            s = lax.dot_general(qh, k_refs[p][:, hs], (((1,), (1,)), ((), ())), preferred_element_type=F32)
            s = s * scale + bias_ref[h, :, p * pw:(p + 1) * pw]
            if mask_parts and p < n_parts - 1:
                s = jnp.where(blk - (n_parts - 1) + p >= 0, s, -jnp.inf)
            s_parts.append(s)
        m = s_parts[0].max(axis=-1, keepdims=True)
        for s in s_parts[1:]:
            m = jnp.maximum(m, s.max(axis=-1, keepdims=True))
        l = None
        o = None
        for p in range(n_parts):
            e = jnp.exp(s_parts[p] - m)
            ls = jnp.sum(e, axis=-1, keepdims=True)
            ov = jnp.dot(e.astype(BF16), v_refs[p][:, hs], preferred_element_type=F32)
            l = ls if l is None else l + ls
            o = ov if o is None else o + ov
        o_ref[:, hs] = (o / l).astype(o_ref.dtype)


def attn_prompt_call(q, kv, bias, n_rows, n_heads, head_dim):
    D = n_heads * head_dim
    tq = ATTN_GROUP * CHUNK
    n_parts = (N_BACK_CHUNKS * CHUNK) // tq + 1
    kern = functools.partial(_attn_kernel, n_parts=n_parts, n_heads=n_heads, head_dim=head_dim, mask_parts=True)
    kspec = lambda p, col: pl.BlockSpec(
        (tq, D), lambda b, p=p, col=col: (jnp.maximum(b - (n_parts - 1) + p, 0), col))
    return pl.pallas_call(
        kern,
        grid=(n_rows // tq,),
        in_specs=[pl.BlockSpec((tq, D), lambda b: (b, 0))]
        + [kspec(p, 0) for p in range(n_parts)] + [kspec(p, 1) for p in range(n_parts)]
        + [pl.BlockSpec(bias.shape, lambda b: (0, 0, 0), pipeline_mode=pl.Buffered(1))],
        out_specs=pl.BlockSpec((tq, D), lambda b: (b, 0)),
        out_shape=jax.ShapeDtypeStruct((n_rows, D), BF16),
        compiler_params=_params(("arbitrary",)),
        name="attn_prompt",
    )(q, *([kv] * (2 * n_parts)), bias)


def attn_step_call(q, k_all, v_all, bias, n_heads, head_dim):
    B, nk, D = k_all.shape
    S = q.shape[0] // B
    kern = functools.partial(_attn_kernel, n_parts=1, n_heads=n_heads, head_dim=head_dim, mask_parts=False)
    return pl.pallas_call(
        kern,
        grid=(B,),
        in_specs=[pl.BlockSpec((S, D), lambda b: (b, 0)),
                  pl.BlockSpec((None, nk, D), lambda b: (b, 0, 0)),
                  pl.BlockSpec((None, nk, D), lambda b: (b, 0, 0)),
                  pl.BlockSpec(bias.shape, lambda b: (0, 0, 0))],
        out_specs=pl.BlockSpec((S, D), lambda b: (b, 0)),
        out_shape=jax.ShapeDtypeStruct((B * S, D), BF16),
        compiler_params=_params(("arbitrary",)),
        name="attn_step",
    )(q, k_all, v_all, bias)


def band_bias(table):
    tq = ATTN_GROUP * CHUNK
    nk = tq + N_BACK_CHUNKS * CHUNK
    i = jnp.arange(tq)[:, None]
    j = jnp.arange(nk)[None, :]
    dist = N_BACK_CHUNKS * CHUNK + i - j
    idx = jnp.clip(dist, -REL_CLIP, REL_CLIP) + REL_CLIP
    qc, jc = i // CHUNK, j // CHUNK
    band = (jc >= qc) & (jc <= qc + N_BACK_CHUNKS)
    return jnp.where(band[None], table[:, idx].astype(F32), -jnp.inf)


def kernel(x_prompt, x_sample, state_conv, cache_k, cache_v, g_mix, g_ffn, w_in_a, conv_a, w_out_a, g_kv, w_kv,
           w_q_b, w_o_b, rel_bias_b, w_group, b_group, w_router, b_router, w_gate, w_up, w_down, g_final):
    Bp, Tp, D = x_prompt.shape
    Bs, Ss, _ = x_sample.shape
    n_a = w_in_a.shape[0]
    depth = g_mix.shape[0]
    n_heads, head_dim = cache_k.shape[2], cache_k.shape[3]
    n_cache = cache_k.shape[1]
    assert Bp == 1 and Ss == CHUNK
    n_p = Bp * Tp
    n_s = Bs * Ss
    M = n_p + n_s
    n_tiles_max = (2 * M) // EXPERT_TILE + N_EXPERTS
    n_slots = n_tiles_max * EXPERT_TILE

    x = jnp.concatenate([x_prompt.reshape(n_p, D), x_sample.reshape(n_s, D)], axis=0)

    def moe(x1, xn2, route, counts, layer, gains, gain_dtypes, emit_x):
        dest, tile_expert, n_tiles, pad_start, pad_len = routing_plan(route, counts, n_tiles_max)
        xs = dispatch_call(xn2, dest, pad_start, pad_len, n_tiles, n_slots)
        ys = experts_call(xs, tile_expert, n_tiles, w_gate, w_up, w_down, layer)
        return combine_call(x1, route, ys, dest, gains, gain_dtypes, emit_x)

    def router_weights(layer):
        pad = LANES - N_GROUPS - N_EXPERTS
        w = jnp.concatenate([w_group[layer], w_router[layer], jnp.zeros((D, pad), F32)], axis=1)
        b = jnp.concatenate([b_group[layer], b_router[layer], jnp.zeros((pad,), F32)]).reshape(1, LANES)
        return w, b

    xn = rmsnorm_call(x, g_mix[0], BF16)
    conv_p, conv_s = [], []
    for l in range(n_a):
        st = state_conv[l]
        z = jnp.zeros((Bs, Ss - 1, D), F32)
        inj1 = jnp.concatenate([st[:, 1:2], z], axis=1).reshape(n_s, D)
        inj2 = jnp.concatenate([st, z[:, 1:]], axis=1).reshape(n_s, D)
        g, cp, cs = conv_in_call(xn, w_in_a, conv_a, l, inj1, inj2, n_p, Bs, Ss)
        conv_p.append(cp.reshape(Bp, 2, D))
        conv_s.append(cs)
        wr, br = router_weights(l)
        x1, xn2, route, counts = proj_route_call(g, w_out_a, (l,), x, g_ffn[l], wr, br)
        if l + 1 < n_a:
            x, xn = moe(x1, xn2, route, counts, l, [g_mix[l + 1]], [BF16], True)
        else:
            x, xn_kv, xn = moe(x1, xn2, route, counts, l, [g_kv, g_mix[l + 1]], [BF16, BF16], True)

    kv_bf, kv_f = matmul_call(xn_kv, w_kv, (), [BF16, F32], name="kv_proj")
    k_new = kv_f[:, :D]
    v_new = kv_f[:, D:]
    k_all = jnp.concatenate([cache_k.reshape(Bs, n_cache, D), k_new[n_p:].reshape(Bs, Ss, D)], axis=1)
    v_all = jnp.concatenate([cache_v.reshape(Bs, n_cache, D), v_new[n_p:].reshape(Bs, Ss, D)], axis=1)
    k_all_bf = k_all.astype(BF16)
    v_all_bf = v_all.astype(BF16)

    y = None
    for j in range(depth - n_a):
        l = n_a + j
        (q,) = matmul_call(xn, w_q_b, (j,), [BF16], name="q_proj")
        bias = band_bias(rel_bias_b[j])
        o_p = attn_prompt_call(q, kv_bf, bias, n_p, n_heads, head_dim)
        o_s = attn_step_call(q[n_p:], k_all_bf, v_all_bf, bias[:, :Ss, :n_cache + Ss], n_heads, head_dim)
        o = jnp.concatenate([o_p, o_s], axis=0)
        wr, br = router_weights(l)
        x1, xn2, route, counts = proj_route_call(o, w_o_b, (j,), x, g_ffn[l], wr, br)
        if l + 1 < depth:
            x, xn = moe(x1, xn2, route, counts, l, [g_mix[l + 1]], [BF16], True)
        else:
            (y,) = moe(x1, xn2, route, counts, l, [g_final], [F32], False)

    keep_p = min(N_BACK_CHUNKS * CHUNK, Tp)
    y_prompt = y[:n_p].reshape(Bp, Tp, D)
    y_sample = y[n_p:].reshape(Bs, Ss, D)
    new_conv_prompt = jnp.stack(conv_p, axis=0)
    new_conv_sample = jnp.stack(conv_s, axis=0)
    new_k_prompt = k_new[n_p - keep_p:n_p].reshape(Bp, keep_p, n_heads, head_dim)
    new_v_prompt = v_new[n_p - keep_p:n_p].reshape(Bp, keep_p, n_heads, head_dim)
    new_k_sample = k_all[:, Ss:].reshape(Bs, n_cache, n_heads, head_dim)
    new_v_sample = v_all[:, Ss:].reshape(Bs, n_cache, n_heads, head_dim)
    return (y_prompt, y_sample, new_conv_prompt, new_k_prompt, new_v_prompt,
            new_conv_sample, new_k_sample, new_v_sample)
```

```python
import functools

import jax
import jax.numpy as jnp
from jax import lax
from jax.experimental import pallas as pl
from jax.experimental.pallas import tpu as pltpu

F32 = jnp.float32
BF16 = jnp.bfloat16

CHUNK = 64
N_BACK_CHUNKS = 8
REL_CLIP = 128
N_GROUPS = 4
EXPERTS_PER_GROUP = 4
N_EXPERTS = N_GROUPS * EXPERTS_PER_GROUP
EPS = 1e-6

LANES = 128
EXPERT_TILE = 256
ATTN_GROUP = 4
DMA_UNROLL = 8
VMEM_LIMIT = 56 * 1024 * 1024


def _params(sem, **kw):
    return pltpu.CompilerParams(dimension_semantics=sem, vmem_limit_bytes=VMEM_LIMIT, **kw)


def _rms(x):
    return x * lax.rsqrt(jnp.mean(x * x, axis=-1, keepdims=True) + EPS)


def _norm_kernel(x_ref, g_ref, o_ref):
    o_ref[...] = (_rms(x_ref[...]) * g_ref[...]).astype(o_ref.dtype)


def rmsnorm_call(x, g, dtype, tm=512):
    M, D = x.shape
    return pl.pallas_call(
        _norm_kernel,
        grid=(M // tm,),
        in_specs=[pl.BlockSpec((tm, D), lambda i: (i, 0)), pl.BlockSpec((1, D), lambda i: (0, 0))],
        out_specs=pl.BlockSpec((tm, D), lambda i: (i, 0)),
        out_shape=jax.ShapeDtypeStruct((M, D), dtype),
        compiler_params=_params(("arbitrary",)),
        name="rmsnorm",
    )(x, g.reshape(1, D))


def _conv_in_kernel(xn_ref, wb_ref, wc_ref, wh_ref, cw_ref, inj1_ref, inj2_ref,
                    g_ref, cp_ref, cs_ref, wbs, wcs, whs, carry, u_s, *, n_prompt_tiles, n_seq, seq_len):
    i = pl.program_id(1)
    tm = u_s.shape[0]

    @pl.when(i == 0)
    def _():
        wbs[...] = wb_ref[...].astype(BF16)
        wcs[...] = wc_ref[...].astype(BF16)
        whs[...] = wh_ref[...].astype(BF16)
        carry[...] = jnp.zeros_like(carry)

    xn = xn_ref[...]
    b = jnp.dot(xn, wbs[...], preferred_element_type=F32)
    c = jnp.dot(xn, wcs[...], preferred_element_type=F32)
    h = jnp.dot(xn, whs[...], preferred_element_type=F32)
    u = c * h
    u_s[...] = u
    is_sample = i >= n_prompt_tiles
    row = lax.broadcasted_iota(jnp.int32, u.shape, 0)
    rowm = row & jnp.where(is_sample, seq_len - 1, 0x7FFFFFFF)
    c0 = carry[0:1, :]
    c1 = carry[1:2, :]
    e1 = jnp.where(is_sample, inj1_ref[...], c1)
    e2 = jnp.where(is_sample, inj2_ref[...], jnp.where(row == 0, c0, c1))
    up1 = jnp.where(rowm == 0, e1, pltpu.roll(u, 1, 0))
    up2 = jnp.where(rowm < 2, e2, pltpu.roll(u, 2, 0))
    cw = cw_ref[...]
    conv = cw[0:1, :] * up2 + cw[1:2, :] * up1 + cw[2:3, :] * u
    g_ref[...] = (b * conv).astype(g_ref.dtype)
    carry[...] = u_s[tm - 2:tm, :]

    @pl.when(i == n_prompt_tiles - 1)
    def _():
        cp_ref[...] = u_s[tm - 2:tm, :]

    @pl.when(i == n_prompt_tiles)
    def _():
        for s in range(n_seq):
            cs_ref[s] = u_s[(s + 1) * seq_len - 2:(s + 1) * seq_len, :]


def conv_in_call(xn, w_in_all, conv_all, layer, inj1, inj2, n_prompt_rows, n_seq, seq_len, tn=512):
    M, D = xn.shape
    tm = n_seq * seq_len
    assert M == n_prompt_rows + tm and n_prompt_rows % tm == 0 and D % tn == 0
    nj, ni = D // tn, M // tm
    kern = functools.partial(_conv_in_kernel, n_prompt_tiles=n_prompt_rows // tm, n_seq=n_seq, seq_len=seq_len)
    wspec = lambda k: pl.BlockSpec((None, D, tn), lambda j, i, k=k: (layer, 0, j + k * nj))
    return pl.pallas_call(
        kern,
        grid=(nj, ni),
        in_specs=[
            pl.BlockSpec((tm, D), lambda j, i: (i, 0)),
            wspec(0), wspec(1), wspec(2),
            pl.BlockSpec((None, 3, tn), lambda j, i: (layer, 0, j)),
            pl.BlockSpec((tm, tn), lambda j, i: (0, j)),
            pl.BlockSpec((tm, tn), lambda j, i: (0, j)),
        ],
        out_specs=[
            pl.BlockSpec((tm, tn), lambda j, i: (i, j)),
            pl.BlockSpec((2, tn), lambda j, i: (0, j)),
            pl.BlockSpec((n_seq, 2, tn), lambda j, i: (0, 0, j)),
        ],
        out_shape=[
            jax.ShapeDtypeStruct((M, D), BF16),
            jax.ShapeDtypeStruct((2, D), F32),
            jax.ShapeDtypeStruct((n_seq, 2, D), F32),
        ],
        scratch_shapes=[pltpu.VMEM((D, tn), BF16)] * 3 + [pltpu.VMEM((2, tn), F32), pltpu.VMEM((tm, tn), F32)],
        compiler_params=_params(("arbitrary", "arbitrary")),
        name="conv_in",
    )(xn, w_in_all, w_in_all, w_in_all, conv_all, inj1, inj2)


def _matmul_kernel(x_ref, w_ref, *rest):
    out_refs, ws = rest[:-1], rest[-1]

    @pl.when(pl.program_id(1) == 0)
    def _():
        ws[...] = w_ref[...].astype(BF16)

    y = jnp.dot(x_ref[...], ws[...], preferred_element_type=F32)
    for o in out_refs:
        o[...] = y.astype(o.dtype)


def matmul_call(x, w, w_index, out_dtypes, tm=512, tn=512, name="matmul"):
    M, K = x.shape
    N = w.shape[-1]
    lead = tuple(w_index)
    wblock = (None,) * len(lead) + (K, tn)
    return pl.pallas_call(
        _matmul_kernel,
        grid=(N // tn, M // tm),
        in_specs=[pl.BlockSpec((tm, K), lambda j, i: (i, 0)),
                  pl.BlockSpec(wblock, lambda j, i: lead + (0, j))],
        out_specs=[pl.BlockSpec((tm, tn), lambda j, i: (i, j)) for _ in out_dtypes],
        out_shape=[jax.ShapeDtypeStruct((M, N), dt) for dt in out_dtypes],
        scratch_shapes=[pltpu.VMEM((K, tn), BF16)],
        compiler_params=_params(("arbitrary", "arbitrary")),
        name=name,
    )(x, w)


def _route(logits, carry):
    tm = logits.shape[0]
    lane = lax.broadcasted_iota(jnp.int32, logits.shape, 1).astype(F32)
    neg = -jnp.inf
    big = float(LANES)
    gl = jnp.where(lane < N_GROUPS, logits, neg)
    gmax = jnp.max(gl, axis=-1, keepdims=True)
    gidx = jnp.min(jnp.where(gl == gmax, lane, big), axis=-1, keepdims=True)
    gp = 1.0 / jnp.sum(jnp.exp(gl - gmax), axis=-1, keepdims=True)
    lo = N_GROUPS + gidx * EXPERTS_PER_GROUP
    el = jnp.where(lane >= lo, jnp.where(lane < lo + EXPERTS_PER_GROUP, logits, neg), neg)
    t1 = jnp.max(el, axis=-1, keepdims=True)
    i1 = jnp.min(jnp.where(el == t1, lane, big), axis=-1, keepdims=True)
    el2 = jnp.where(lane == i1, neg, el)
    t2 = jnp.max(el2, axis=-1, keepdims=True)
    i2 = jnp.min(jnp.where(el2 == t2, lane, big), axis=-1, keepdims=True)
    ex = jnp.exp(t2 - t1)
    den = 1.0 + ex
    w0 = (1.0 / den) * gp
    w1 = (ex / den) * gp
    hit1 = lane == i1
    hit2 = lane == i2
    cnt = jnp.where(hit1, 1.0, jnp.where(hit2, 1.0, 0.0))
    r_i = lax.broadcasted_iota(jnp.int32, (tm, tm), 0)
    c_i = lax.broadcasted_iota(jnp.int32, (tm, tm), 1)
    tri = jnp.where(c_i < r_i, 1.0, 0.0).astype(BF16)
    before = jnp.dot(tri, cnt.astype(BF16), preferred_element_type=F32) + carry[...]
    rank0 = jnp.sum(jnp.where(hit1, before, 0.0), axis=-1, keepdims=True)
    rank1 = jnp.sum(jnp.where(hit2, before, 0.0), axis=-1, keepdims=True)
    carry[...] = carry[...] + jnp.sum(cnt, axis=0, keepdims=True)
    vals = (i1 - N_GROUPS, i2 - N_GROUPS, rank0, rank1, w0, w1)
    route = jnp.zeros_like(logits)
    for k, v in enumerate(vals):
        route = jnp.where(lane == float(k), v, route)
    return route


def _proj_route_kernel(a_ref, w_ref, x_ref, gf_ref, wr_ref, br_ref,
                       x1_ref, xn_ref, route_ref, cnt_ref, ws, carry):
    K = ws.shape[0]

    @pl.when(pl.program_id(0) == 0)
    def _():
        step = 256

        def body(k, _):
            r = pl.multiple_of(k * step, step)
            ws[pl.ds(r, step), :] = w_ref[pl.ds(r, step), :].astype(BF16)
            return 0

        lax.fori_loop(0, K // step, body, 0)
        carry[...] = jnp.zeros_like(carry)

    y = jnp.dot(a_ref[...], ws[...], preferred_element_type=F32)
    x1 = x_ref[...] + y
    x1_ref[...] = x1
    xn = _rms(x1) * gf_ref[...]
    xn_ref[...] = xn
    xh = xn.astype(BF16)
    xl = (xn - xh.astype(F32)).astype(BF16)
    p1 = jnp.dot(xh, wr_ref[...], preferred_element_type=F32)
    p2 = jnp.dot(xl, wr_ref[:, :LANES], preferred_element_type=F32)
    logits = p1[:, :LANES] + (p1[:, LANES:] + p2) + br_ref[...]
    route_ref[...] = _route(logits, carry)
    cnt_ref[...] = carry[...]


def proj_route_call(a, w_all, w_index, x, g_ffn, w_route, b_route, tm=256):
    M, K = a.shape
    D = x.shape[1]
    lead = tuple(w_index)
    wblock = (None,) * len(lead) + (K, D)
    const = lambda i: (0, 0)
    rowblk = lambda i: (i, 0)
    return pl.pallas_call(
        _proj_route_kernel,
        grid=(M // tm,),
        in_specs=[
            pl.BlockSpec((tm, K), rowblk),
            pl.BlockSpec(wblock, lambda i: lead + (0, 0), pipeline_mode=pl.Buffered(1)),
            pl.BlockSpec((tm, D), rowblk),
            pl.BlockSpec((1, D), const),
            pl.BlockSpec((D, 2 * LANES), const),
            pl.BlockSpec((1, LANES), const),
        ],
        out_specs=[
            pl.BlockSpec((tm, D), rowblk),
            pl.BlockSpec((tm, D), rowblk),
            pl.BlockSpec((tm, LANES), rowblk),
            pl.BlockSpec((1, LANES), const),
        ],
        out_shape=[
            jax.ShapeDtypeStruct((M, D), F32),
            jax.ShapeDtypeStruct((M, D), F32),
            jax.ShapeDtypeStruct((M, LANES), F32),
            jax.ShapeDtypeStruct((1, LANES), F32),
        ],
        scratch_shapes=[pltpu.VMEM((K, D), BF16), pltpu.VMEM((1, LANES), F32)],
        compiler_params=_params(("arbitrary",)),
        name="proj_route",
    )(a, w_all, x, g_ffn.reshape(1, D), w_route, b_route)


def routing_plan(route, counts, n_tiles_max):
    te = EXPERT_TILE
    eid = route[:, 0:2].astype(jnp.int32)
    rank = route[:, 2:4].astype(jnp.int32)
    cnt = counts[0, N_GROUPS:N_GROUPS + N_EXPERTS].astype(jnp.int32)
    tiles = (cnt + te - 1) // te
    tile_end = jnp.cumsum(tiles)
    offs = (tile_end - tiles) * te
    onehot = eid[..., None] == jnp.arange(N_EXPERTS, dtype=jnp.int32)
    dest = jnp.sum(jnp.where(onehot, offs, 0), axis=-1) + rank
    n_tiles = tile_end[-1]
    tid = jnp.arange(n_tiles_max, dtype=jnp.int32)
    tile_expert = jnp.sum((jnp.minimum(tid, n_tiles - 1)[:, None] >= tile_end[None, :]).astype(jnp.int32), axis=1)
    pad_start = offs + cnt
    pad_len = tiles * te - cnt
    return dest.reshape(-1), tile_expert, n_tiles.reshape(1), pad_start, pad_len


def _dispatch_kernel(dest_ref, pstart_ref, plen_ref, nt_ref, xn_ref, xs_hbm, zbuf, sem, zsem, tsem, *, tm):
    i = pl.program_id(0)
    base = i * tm
    te = zbuf.shape[0]
    n_tiles_max = xs_hbm.shape[0] // te

    def row_copy(r, slot):
        return pltpu.make_async_copy(xn_ref.at[pl.ds(r, 1), :], xs_hbm.at[pl.ds(slot, 1), :], sem)

    def zero_copy(slot):
        return pltpu.make_async_copy(zbuf.at[pl.ds(0, 1), :], xs_hbm.at[pl.ds(slot, 1), :], zsem)

    def zero_tile_copy(tile):
        return pltpu.make_async_copy(zbuf, xs_hbm.at[pl.ds(pl.multiple_of(tile * te, te), te), :], tsem)

    def issue(k, _):
        for u in range(DMA_UNROLL):
            r = k * DMA_UNROLL + u
            row_copy(r, dest_ref[2 * (base + r)]).start()
            row_copy(r, dest_ref[2 * (base + r) + 1]).start()
        return 0

    lax.fori_loop(0, tm // DMA_UNROLL, issue, 0)

    @pl.when(i == 0)
    def _():
        zbuf[...] = jnp.zeros_like(zbuf)
        for e in range(N_EXPERTS):
            def zissue(r, _, e=e):
                zero_copy(pstart_ref[e] + r).start()
                return 0

            lax.fori_loop(0, plen_ref[e], zissue, 0)

        def tissue(t, _):
            zero_tile_copy(t).start()
            return 0

        lax.fori_loop(nt_ref[0], n_tiles_max, tissue, 0)
        for e in range(N_EXPERTS):
            def zwait(r, _):
                zero_copy(0).wait()
                return 0

            lax.fori_loop(0, plen_ref[e], zwait, 0)

        def twait(t, _):
            zero_tile_copy(0).wait()
            return 0

        lax.fori_loop(nt_ref[0], n_tiles_max, twait, 0)

    def wait(k, _):
        for u in range(2 * DMA_UNROLL):
            row_copy(0, 0).wait()
        return 0

    lax.fori_loop(0, tm // DMA_UNROLL, wait, 0)


def dispatch_call(xn, dest, pad_start, pad_len, n_tiles, n_slots, tm=512):
    M, D = xn.shape
    kern = functools.partial(_dispatch_kernel, tm=tm)
    return pl.pallas_call(
        kern,
        grid_spec=pltpu.PrefetchScalarGridSpec(
            num_scalar_prefetch=4,
            grid=(M // tm,),
            in_specs=[pl.BlockSpec((tm, D), lambda i, *_: (i, 0))],
            out_specs=pl.BlockSpec(memory_space=pl.ANY),
            scratch_shapes=[pltpu.VMEM((EXPERT_TILE, D), xn.dtype)] + [pltpu.SemaphoreType.DMA(())] * 3,
        ),
        out_shape=jax.ShapeDtypeStruct((n_slots, D), xn.dtype),
        compiler_params=_params(("arbitrary",)),
        name="dispatch",
    )(dest, pad_start, pad_len, n_tiles, xn)


def _experts_kernel(te_ref, nt_ref, xs_ref, wg_ref, wu_ref, wd_ref, ys_ref, wgs, wus, wds):
    t = pl.program_id(0)
    valid = t < nt_ref[0]
    prev = te_ref[jnp.maximum(t - 1, 0)]
    fresh = jnp.logical_or(t == 0, te_ref[t] != prev)

    @pl.when(jnp.logical_and(valid, fresh))
    def _():
        wgs[...] = wg_ref[...].astype(BF16)
        wus[...] = wu_ref[...].astype(BF16)
        wds[...] = wd_ref[...].astype(BF16)

    @pl.when(valid)
    def _():
        x = xs_ref[...].astype(BF16)
        hg = jnp.dot(x, wgs[...], preferred_element_type=F32)
        hu = jnp.dot(x, wus[...], preferred_element_type=F32)
        h = (hg * (1.0 / (1.0 + jnp.exp(-hg)))) * hu
        ys_ref[...] = jnp.dot(h.astype(BF16), wds[...], preferred_element_type=F32).astype(ys_ref.dtype)

    @pl.when(jnp.logical_not(valid))
    def _():
        ys_ref[...] = jnp.zeros_like(ys_ref)


def experts_call(xs, tile_expert, n_tiles, w_gate, w_up, w_down, layer):
    S, D = xs.shape
    Fd = w_gate.shape[-1]
    te = EXPERT_TILE
    row = lambda t, te_ref, nt_ref: (jnp.minimum(t, nt_ref[0] - 1), 0)
    wsel = lambda t, te_ref, nt_ref: (layer, te_ref[t], 0, 0)
    return pl.pallas_call(
        _experts_kernel,
        grid_spec=pltpu.PrefetchScalarGridSpec(
            num_scalar_prefetch=2,
            grid=(S // te,),
            in_specs=[
                pl.BlockSpec((te, D), row),
                pl.BlockSpec((None, None, D, Fd), wsel),
                pl.BlockSpec((None, None, D, Fd), wsel),
                pl.BlockSpec((None, None, Fd, D), wsel),
            ],
            out_specs=pl.BlockSpec((te, D), lambda t, te_ref, nt_ref: (t, 0)),
            scratch_shapes=[pltpu.VMEM((D, Fd), BF16), pltpu.VMEM((D, Fd), BF16), pltpu.VMEM((Fd, D), BF16)],
        ),
        out_shape=jax.ShapeDtypeStruct((S, D), F32),
        compiler_params=_params(("arbitrary",)),
        name="experts",
    )(tile_expert, n_tiles, xs, w_gate, w_up, w_down)


def _combine_kernel(dest_ref, x1_ref, route_ref, ys_hbm, *rest, tm, n_gain, emit_x):
    gain_refs = rest[:n_gain]
    n_out = n_gain + (1 if emit_x else 0)
    out_refs = rest[n_gain:n_gain + n_out]
    buf0, buf1, sem = rest[n_gain + n_out:]
    base = pl.program_id(0) * tm

    def row_copy(slot, buf, r):
        return pltpu.make_async_copy(ys_hbm.at[pl.ds(slot, 1), :], buf.at[pl.ds(r, 1), :], sem)

    def issue(k, _):
        for u in range(DMA_UNROLL):
            r = k * DMA_UNROLL + u
            row_copy(dest_ref[2 * (base + r)], buf0, r).start()
            row_copy(dest_ref[2 * (base + r) + 1], buf1, r).start()
        return 0

    lax.fori_loop(0, tm // DMA_UNROLL, issue, 0)

    def wait(k, _):
        for u in range(DMA_UNROLL):
            row_copy(0, buf0, 0).wait()
            row_copy(0, buf1, 0).wait()
        return 0

    lax.fori_loop(0, tm // DMA_UNROLL, wait, 0)

    route = route_ref[...]
    x2 = x1_ref[...] + (route[:, 4:5] * buf0[...] + route[:, 5:6] * buf1[...])
    k = 0
    if emit_x:
        out_refs[0][...] = x2
        k = 1
    if n_gain:
        xh = _rms(x2)
        for g_ref, o_ref in zip(gain_refs, out_refs[k:]):
            o_ref[...] = (xh * g_ref[...]).astype(o_ref.dtype)


def combine_call(x1, route, ys, dest, gains, gain_dtypes, emit_x, tm=256):
    M, D = x1.shape
    n_gain = len(gains)
    kern = functools.partial(_combine_kernel, tm=tm, n_gain=n_gain, emit_x=emit_x)
    rowblk = lambda i, d: (i, 0)
    const = lambda i, d: (0, 0)
    out_dtypes = ([F32] if emit_x else []) + list(gain_dtypes)
    return pl.pallas_call(
        kern,
        grid_spec=pltpu.PrefetchScalarGridSpec(
            num_scalar_prefetch=1,
            grid=(M // tm,),
            in_specs=[pl.BlockSpec((tm, D), rowblk), pl.BlockSpec((tm, LANES), rowblk),
                      pl.BlockSpec(memory_space=pl.ANY)] + [pl.BlockSpec((1, D), const)] * n_gain,
            out_specs=[pl.BlockSpec((tm, D), rowblk) for _ in out_dtypes],
            scratch_shapes=[pltpu.VMEM((tm, D), ys.dtype), pltpu.VMEM((tm, D), ys.dtype),
                            pltpu.SemaphoreType.DMA(())],
        ),
        out_shape=[jax.ShapeDtypeStruct((M, D), dt) for dt in out_dtypes],
        compiler_params=_params(("arbitrary",)),
        name="combine",
    )(dest, x1, route, ys, *[g.reshape(1, D) for g in gains])


def _attn_kernel(q_ref, *rest, n_parts, n_heads, head_dim, mask_parts):
    k_refs = rest[:n_parts]
    v_refs = rest[n_parts:2 * n_parts]
    bias_ref, o_ref = rest[2 * n_parts:]
    blk = pl.program_id(0)
    pw = k_refs[0].shape[0]
    scale = head_dim ** -0.5
    for h in range(n_heads):
        hs = slice(h * head_dim, (h + 1) * head_dim)
        qh = q_ref[:, hs]
        s_parts = []
        for p in range(n_parts):
            s = lax.dot_general(qh, k_refs[p][:, hs], (((1,), (1,)), ((), ())), preferred_element_type=F32)
            s = s * scale + bias_ref[h, :, p * pw:(p + 1) * pw]
            if mask_parts and p < n_parts - 1:
                s = jnp.where(blk - (n_parts - 1) + p >= 0, s, -jnp.inf)
            s_parts.append(s)
        m = s_parts[0].max(axis=-1, keepdims=True)
        for s in s_parts[1:]:
            m = jnp.maximum(m, s.max(axis=-1, keepdims=True))
        l = None
        o = None
        for p in range(n_parts):
            e = jnp.exp(s_parts[p] - m)
            ls = jnp.sum(e, axis=-1, keepdims=True)
            ov = jnp.dot(e.astype(BF16), v_refs[p][:, hs], preferred_element_type=F32)
            l = ls if l is None else l + ls
            o = ov if o is None else o + ov
        o_ref[:, hs] = (o / l).astype(o_ref.dtype)


def attn_prompt_call(q, kv, bias, n_rows, n_heads, head_dim):
    D = n_heads * head_dim
    tq = ATTN_GROUP * CHUNK
    n_parts = (N_BACK_CHUNKS * CHUNK) // tq + 1
    kern = functools.partial(_attn_kernel, n_parts=n_parts, n_heads=n_heads, head_dim=head_dim, mask_parts=True)
    kspec = lambda p, col: pl.BlockSpec(
        (tq, D), lambda b, p=p, col=col: (jnp.maximum(b - (n_parts - 1) + p, 0), col))
    return pl.pallas_call(
        kern,
        grid=(n_rows // tq,),
        in_specs=[pl.BlockSpec((tq, D), lambda b: (b, 0))]
        + [kspec(p, 0) for p in range(n_parts)] + [kspec(p, 1) for p in range(n_parts)]
        + [pl.BlockSpec(bias.shape, lambda b: (0, 0, 0), pipeline_mode=pl.Buffered(1))],
        out_specs=pl.BlockSpec((tq, D), lambda b: (b, 0)),
        out_shape=jax.ShapeDtypeStruct((n_rows, D), BF16),
        compiler_params=_params(("arbitrary",)),
        name="attn_prompt",
    )(q, *([kv] * (2 * n_parts)), bias)


def attn_step_call(q, k_all, v_all, bias, n_heads, head_dim):
    B, nk, D = k_all.shape
    S = q.shape[0] // B
    kern = functools.partial(_attn_kernel, n_parts=1, n_heads=n_heads, head_dim=head_dim, mask_parts=False)
    return pl.pallas_call(
        kern,
        grid=(B,),
        in_specs=[pl.BlockSpec((S, D), lambda b: (b, 0)),
                  pl.BlockSpec((None, nk, D), lambda b: (b, 0, 0)),
                  pl.BlockSpec((None, nk, D), lambda b: (b, 0, 0)),
                  pl.BlockSpec(bias.shape, lambda b: (0, 0, 0))],
        out_specs=pl.BlockSpec((S, D), lambda b: (b, 0)),
        out_shape=jax.ShapeDtypeStruct((B * S, D), BF16),
        compiler_params=_params(("arbitrary",)),
        name="attn_step",
    )(q, k_all, v_all, bias)


def band_bias(table):
    H = table.shape[0]
    tq = ATTN_GROUP * CHUNK
    nk = tq + N_BACK_CHUNKS * CHUNK
    L = tq + nk - 1
    s = jnp.arange(L)
    dist = s + (N_BACK_CHUNKS * CHUNK - (nk - 1))
    f = table[:, jnp.clip(dist, -REL_CLIP, REL_CLIP) + REL_CLIP].astype(F32)
    w = jnp.concatenate([f[:, tq - 1:], jnp.zeros((H, 1), F32), f[:, :tq - 1]], axis=1)
    g = jnp.tile(w, (1, tq))[:, :tq * L].reshape(H, tq, L)[:, :, :nk]
    b = g[:, ::-1, ::-1]
    i = jnp.arange(tq)[:, None]
    j = jnp.arange(nk)[None, :]
    qc, jc = i // CHUNK, j // CHUNK
    band = (jc >= qc) & (jc <= qc + N_BACK_CHUNKS)
    return jnp.where(band[None], b, -jnp.inf)


def kernel(x_prompt, x_sample, state_conv, cache_k, cache_v, g_mix, g_ffn, w_in_a, conv_a, w_out_a, g_kv, w_kv,
           w_q_b, w_o_b, rel_bias_b, w_group, b_group, w_router, b_router, w_gate, w_up, w_down, g_final):
    Bp, Tp, D = x_prompt.shape
    Bs, Ss, _ = x_sample.shape
    n_a = w_in_a.shape[0]
    depth = g_mix.shape[0]
    n_heads, head_dim = cache_k.shape[2], cache_k.shape[3]
    n_cache = cache_k.shape[1]
    assert Bp == 1 and Ss == CHUNK
    n_p = Bp * Tp
    n_s = Bs * Ss
    M = n_p + n_s
    n_tiles_max = (2 * M) // EXPERT_TILE + N_EXPERTS
    n_slots = n_tiles_max * EXPERT_TILE

    x = jnp.concatenate([x_prompt.reshape(n_p, D), x_sample.reshape(n_s, D)], axis=0)

    def moe(x1, xn2, route, counts, layer, gains, gain_dtypes, emit_x):
        dest, tile_expert, n_tiles, pad_start, pad_len = routing_plan(route, counts, n_tiles_max)
        xs = dispatch_call(xn2, dest, pad_start, pad_len, n_tiles, n_slots)
        ys = experts_call(xs, tile_expert, n_tiles, w_gate, w_up, w_down, layer)
        return combine_call(x1, route, ys, dest, gains, gain_dtypes, emit_x)

    def router_weights(layer):
        pad = LANES - N_GROUPS - N_EXPERTS
        w = jnp.concatenate([w_group[layer], w_router[layer], jnp.zeros((D, pad), F32)], axis=1)
        w_hi = w.astype(BF16)
        w_lo = (w - w_hi.astype(F32)).astype(BF16)
        w = jnp.concatenate([w_hi, w_lo], axis=1)
        b = jnp.concatenate([b_group[layer], b_router[layer], jnp.zeros((pad,), F32)]).reshape(1, LANES)
        return w, b

    xn = rmsnorm_call(x, g_mix[0], BF16)
    conv_p, conv_s = [], []
    for l in range(n_a):
        st = state_conv[l]
        z = jnp.zeros((Bs, Ss - 1, D), F32)
        inj1 = jnp.concatenate([st[:, 1:2], z], axis=1).reshape(n_s, D)
        inj2 = jnp.concatenate([st, z[:, 1:]], axis=1).reshape(n_s, D)
        g, cp, cs = conv_in_call(xn, w_in_a, conv_a, l, inj1, inj2, n_p, Bs, Ss)
        conv_p.append(cp.reshape(Bp, 2, D))
        conv_s.append(cs)
        wr, br = router_weights(l)
        x1, xn2, route, counts = proj_route_call(g, w_out_a, (l,), x, g_ffn[l], wr, br)
        if l + 1 < n_a:
            x, xn = moe(x1, xn2, route, counts, l, [g_mix[l + 1]], [BF16], True)
        else:
            x, xn_kv, xn = moe(x1, xn2, route, counts, l, [g_kv, g_mix[l + 1]], [BF16, BF16], True)

    kv_bf, kv_f = matmul_call(xn_kv, w_kv, (), [BF16, F32], name="kv_proj")
    k_new = kv_f[:, :D]
    v_new = kv_f[:, D:]
    k_all = jnp.concatenate([cache_k.reshape(Bs, n_cache, D), k_new[n_p:].reshape(Bs, Ss, D)], axis=1)
    v_all = jnp.concatenate([cache_v.reshape(Bs, n_cache, D), v_new[n_p:].reshape(Bs, Ss, D)], axis=1)
    k_all_bf = k_all.astype(BF16)
    v_all_bf = v_all.astype(BF16)

    y = None
    for j in range(depth - n_a):
        l = n_a + j
        (q,) = matmul_call(xn, w_q_b, (j,), [BF16], name="q_proj")
        bias = band_bias(rel_bias_b[j])
        o_p = attn_prompt_call(q, kv_bf, bias, n_p, n_heads, head_dim)
        o_s = attn_step_call(q[n_p:], k_all_bf, v_all_bf, bias[:, :Ss, :n_cache + Ss], n_heads, head_dim)
        o = jnp.concatenate([o_p, o_s], axis=0)
        wr, br = router_weights(l)
        x1, xn2, route, counts = proj_route_call(o, w_o_b, (j,), x, g_ffn[l], wr, br)
        if l + 1 < depth:
            x, xn = moe(x1, xn2, route, counts, l, [g_mix[l + 1]], [BF16], True)
        else:
            (y,) = moe(x1, xn2, route, counts, l, [g_final], [F32], False)

    keep_p = min(N_BACK_CHUNKS * CHUNK, Tp)
    y_prompt = y[:n_p].reshape(Bp, Tp, D)
    y_sample = y[n_p:].reshape(Bs, Ss, D)
    new_conv_prompt = jnp.stack(conv_p, axis=0)
    new_conv_sample = jnp.stack(conv_s, axis=0)
    new_k_prompt = k_new[n_p - keep_p:n_p].reshape(Bp, keep_p, n_heads, head_dim)
    new_v_prompt = v_new[n_p - keep_p:n_p].reshape(Bp, keep_p, n_heads, head_dim)
    new_k_sample = k_all[:, Ss:].reshape(Bs, n_cache, n_heads, head_dim)
    new_v_sample = v_all[:, Ss:].reshape(Bs, n_cache, n_heads, head_dim)
    return (y_prompt, y_sample, new_conv_prompt, new_k_prompt, new_v_prompt,
            new_conv_sample, new_k_sample, new_v_sample)
```

```python
import functools
import math

import jax
import jax.numpy as jnp
from jax import lax
from jax.experimental import pallas as pl
from jax.experimental.pallas import tpu as pltpu

F32 = jnp.float32
BF16 = jnp.bfloat16
U32 = jnp.uint32

CHUNK = 64
N_BACK_CHUNKS = 8
REL_CLIP = 128
N_GROUPS = 4
EXPERTS_PER_GROUP = 4
N_EXPERTS = N_GROUPS * EXPERTS_PER_GROUP
EPS = 1e-6

LANES = 128
SUBLANES = 8
EXPERT_TILE = 256
ATTN_GROUP = 4
DMA_UNROLL = 8
VMEM_LIMIT = 56 * 1024 * 1024
LOG2E = math.log2(math.e)


def _params(sem, **kw):
    return pltpu.CompilerParams(dimension_semantics=sem, vmem_limit_bytes=VMEM_LIMIT, **kw)


def _rms(x):
    return x * lax.rsqrt(jnp.mean(x * x, axis=-1, keepdims=True) + EPS)


def _pack_store(ref, v):
    n, d = v.shape
    half = d // 2
    for c in range(half // LANES):
        lo = v[:, c * LANES:(c + 1) * LANES].astype(BF16).astype(F32)
        hi = v[:, half + c * LANES:half + (c + 1) * LANES].astype(BF16).astype(F32)
        ref[pl.ds(c, n, stride=SUBLANES), :] = (pltpu.bitcast(lo, U32) >> 16) | pltpu.bitcast(hi, U32)


def _unpack_load(ref, n):
    los, his = [], []
    for c in range(SUBLANES):
        w = ref[pl.ds(c, n, stride=SUBLANES), :]
        los.append(pltpu.bitcast(w << 16, F32))
        his.append(pltpu.bitcast(w & jnp.uint32(0xFFFF0000), F32))
    return los, his


def _norm_kernel(x_ref, g_ref, o_ref):
    o_ref[...] = (_rms(x_ref[...]) * g_ref[...]).astype(o_ref.dtype)


def rmsnorm_call(x, g, dtype, tm=512):
    M, D = x.shape
    return pl.pallas_call(
        _norm_kernel,
        grid=(M // tm,),
        in_specs=[pl.BlockSpec((tm, D), lambda i: (i, 0)), pl.BlockSpec((1, D), lambda i: (0, 0))],
        out_specs=pl.BlockSpec((tm, D), lambda i: (i, 0)),
        out_shape=jax.ShapeDtypeStruct((M, D), dtype),
        compiler_params=_params(("arbitrary",)),
        name="rmsnorm",
    )(x, g.reshape(1, D))


def _conv_in_kernel(xn_ref, wb_ref, wc_ref, wh_ref, cw_ref, inj1_ref, inj2_ref,
                    g_ref, cp_ref, cs_ref, wbs, wcs, whs, carry, u_s, *, n_prompt_tiles, n_seq, seq_len):
    i = pl.program_id(1)
    tm = u_s.shape[0]

    @pl.when(i == 0)
    def _():
        wbs[...] = wb_ref[...].astype(BF16)
        wcs[...] = wc_ref[...].astype(BF16)
        whs[...] = wh_ref[...].astype(BF16)
        carry[...] = jnp.zeros_like(carry)

    xn = xn_ref[...]
    b = jnp.dot(xn, wbs[...], preferred_element_type=F32)
    c = jnp.dot(xn, wcs[...], preferred_element_type=F32)
    h = jnp.dot(xn, whs[...], preferred_element_type=F32)
    u = c * h
    u_s[...] = u
    is_sample = i >= n_prompt_tiles
    row = lax.broadcasted_iota(jnp.int32, u.shape, 0)
    rowm = row & jnp.where(is_sample, seq_len - 1, 0x7FFFFFFF)
    c0 = carry[0:1, :]
    c1 = carry[1:2, :]
    e1 = jnp.where(is_sample, inj1_ref[...], c1)
    e2 = jnp.where(is_sample, inj2_ref[...], jnp.where(row == 0, c0, c1))
    up1 = jnp.where(rowm == 0, e1, pltpu.roll(u, 1, 0))
    up2 = jnp.where(rowm < 2, e2, pltpu.roll(u, 2, 0))
    cw = cw_ref[...]
    conv = cw[0:1, :] * up2 + cw[1:2, :] * up1 + cw[2:3, :] * u
    g_ref[...] = (b * conv).astype(g_ref.dtype)
    carry[...] = u_s[tm - 2:tm, :]

    @pl.when(i == n_prompt_tiles - 1)
    def _():
        cp_ref[...] = u_s[tm - 2:tm, :]

    @pl.when(i == n_prompt_tiles)
    def _():
        for s in range(n_seq):
            cs_ref[s] = u_s[(s + 1) * seq_len - 2:(s + 1) * seq_len, :]


def conv_in_call(xn, w_in_all, conv_all, layer, inj1, inj2, n_prompt_rows, n_seq, seq_len, tn=512):
    M, D = xn.shape
    tm = n_seq * seq_len
    assert M == n_prompt_rows + tm and n_prompt_rows % tm == 0 and D % tn == 0
    nj, ni = D // tn, M // tm
    kern = functools.partial(_conv_in_kernel, n_prompt_tiles=n_prompt_rows // tm, n_seq=n_seq, seq_len=seq_len)
    wspec = lambda k: pl.BlockSpec((None, D, tn), lambda j, i, k=k: (layer, 0, j + k * nj))
    return pl.pallas_call(
        kern,
        grid=(nj, ni),
        in_specs=[
            pl.BlockSpec((tm, D), lambda j, i: (i, 0)),
            wspec(0), wspec(1), wspec(2),
            pl.BlockSpec((None, 3, tn), lambda j, i: (layer, 0, j)),
            pl.BlockSpec((tm, tn), lambda j, i: (0, j)),
            pl.BlockSpec((tm, tn), lambda j, i: (0, j)),
        ],
        out_specs=[
            pl.BlockSpec((tm, tn), lambda j, i: (i, j)),
            pl.BlockSpec((2, tn), lambda j, i: (0, j)),
            pl.BlockSpec((n_seq, 2, tn), lambda j, i: (0, 0, j)),
        ],
        out_shape=[
            jax.ShapeDtypeStruct((M, D), BF16),
            jax.ShapeDtypeStruct((2, D), F32),
            jax.ShapeDtypeStruct((n_seq, 2, D), F32),
        ],
        scratch_shapes=[pltpu.VMEM((D, tn), BF16)] * 3 + [pltpu.VMEM((2, tn), F32), pltpu.VMEM((tm, tn), F32)],
        compiler_params=_params(("arbitrary", "arbitrary")),
        name="conv_in",
    )(xn, w_in_all, w_in_all, w_in_all, conv_all, inj1, inj2)


def _matmul_kernel(x_ref, w_ref, *rest, out_scale):
    out_refs, ws = rest[:-1], rest[-1]

    @pl.when(pl.program_id(1) == 0)
    def _():
        ws[...] = w_ref[...].astype(BF16)

    y = jnp.dot(x_ref[...], ws[...], preferred_element_type=F32)
    if out_scale is not None:
        y = y * out_scale
    for o in out_refs:
        o[...] = y.astype(o.dtype)


def matmul_call(x, w, w_index, out_dtypes, tail_tiles=(), out_scale=None, tm=512, tn=512, name="matmul"):
    M, K = x.shape
    N = w.shape[-1]
    ni = M // tm
    lead = tuple(w_index)
    wblock = (None,) * len(lead) + (K, tn)
    tails = tuple(tail_tiles) + (0,) * (len(out_dtypes) - len(tail_tiles))
    out_specs, out_shape = [], []
    for dt, n in zip(out_dtypes, tails):
        if n:
            out_specs.append(pl.BlockSpec((tm, tn), lambda j, i, n=n: (jnp.maximum(i - (ni - n), 0), j)))
            out_shape.append(jax.ShapeDtypeStruct((n * tm, N), dt))
        else:
            out_specs.append(pl.BlockSpec((tm, tn), lambda j, i: (i, j)))
            out_shape.append(jax.ShapeDtypeStruct((M, N), dt))
    return pl.pallas_call(
        functools.partial(_matmul_kernel, out_scale=out_scale),
        grid=(N // tn, ni),
        in_specs=[pl.BlockSpec((tm, K), lambda j, i: (i, 0)),
                  pl.BlockSpec(wblock, lambda j, i: lead + (0, j))],
        out_specs=out_specs,
        out_shape=out_shape,
        scratch_shapes=[pltpu.VMEM((K, tn), BF16)],
        compiler_params=_params(("arbitrary", "arbitrary")),
        name=name,
    )(x, w)


def _route(logits, carry):
    tm = logits.shape[0]
    lane = lax.broadcasted_iota(jnp.int32, logits.shape, 1).astype(F32)
    neg = -jnp.inf
    big = float(LANES)
    gl = jnp.where(lane < N_GROUPS, logits, neg)
    gmax = jnp.max(gl, axis=-1, keepdims=True)
    gidx = jnp.min(jnp.where(gl == gmax, lane, big), axis=-1, keepdims=True)
    gp = 1.0 / jnp.sum(jnp.exp(gl - gmax), axis=-1, keepdims=True)
    lo = N_GROUPS + gidx * EXPERTS_PER_GROUP
    el = jnp.where(lane >= lo, jnp.where(lane < lo + EXPERTS_PER_GROUP, logits, neg), neg)
    t1 = jnp.max(el, axis=-1, keepdims=True)
    i1 = jnp.min(jnp.where(el == t1, lane, big), axis=-1, keepdims=True)
    el2 = jnp.where(lane == i1, neg, el)
    t2 = jnp.max(el2, axis=-1, keepdims=True)
    i2 = jnp.min(jnp.where(el2 == t2, lane, big), axis=-1, keepdims=True)
    ex = jnp.exp(t2 - t1)
    den = 1.0 + ex
    w0 = (1.0 / den) * gp
    w1 = (ex / den) * gp
    hit1 = lane == i1
    hit2 = lane == i2
    cnt = jnp.where(hit1, 1.0, jnp.where(hit2, 1.0, 0.0))
    r_i = lax.broadcasted_iota(jnp.int32, (tm, tm), 0)
    c_i = lax.broadcasted_iota(jnp.int32, (tm, tm), 1)
    tri = jnp.where(c_i < r_i, 1.0, 0.0).astype(BF16)
    before = jnp.dot(tri, cnt.astype(BF16), preferred_element_type=F32) + carry[...]
    rank0 = jnp.sum(jnp.where(hit1, before, 0.0), axis=-1, keepdims=True)
    rank1 = jnp.sum(jnp.where(hit2, before, 0.0), axis=-1, keepdims=True)
    carry[...] = carry[...] + jnp.sum(cnt, axis=0, keepdims=True)
    vals = (i1 - N_GROUPS, i2 - N_GROUPS, rank0, rank1, w0, w1)
    route = jnp.zeros_like(logits)
    for k, v in enumerate(vals):
        route = jnp.where(lane == float(k), v, route)
    return route


def _proj_route_kernel(a_ref, w_ref, x_ref, gf_ref, wr_ref, br_ref,
                       x1_ref, xnp_ref, route_ref, routet_ref, cnt_ref, ws, carry):
    K = ws.shape[0]

    @pl.when(pl.program_id(0) == 0)
    def _():
        step = 256

        def body(k, _):
            r = pl.multiple_of(k * step, step)
            ws[pl.ds(r, step), :] = w_ref[pl.ds(r, step), :].astype(BF16)
            return 0

        lax.fori_loop(0, K // step, body, 0)
        carry[...] = jnp.zeros_like(carry)

    y = jnp.dot(a_ref[...], ws[...], preferred_element_type=F32)
    x1 = x_ref[...] + y
    x1_ref[...] = x1
    xn = _rms(x1) * gf_ref[...]
    _pack_store(xnp_ref, xn)
    xh = xn.astype(BF16)
    xl = (xn - xh.astype(F32)).astype(BF16)
    p1 = jnp.dot(xh, wr_ref[...], preferred_element_type=F32)
    p2 = jnp.dot(xl, wr_ref[:, :LANES], preferred_element_type=F32)
    logits = p1[:, :LANES] + (p1[:, LANES:] + p2) + br_ref[...]
    route = _route(logits, carry)
    route_ref[...] = route
    routet_ref[...] = route.T[:SUBLANES, :]
    cnt_ref[...] = carry[...]


def proj_route_call(a, w_all, w_index, x, g_ffn, w_route, b_route, tm=256):
    M, K = a.shape
    D = x.shape[1]
    lead = tuple(w_index)
    wblock = (None,) * len(lead) + (K, D)
    const = lambda i: (0, 0)
    rowblk = lambda i: (i, 0)
    return pl.pallas_call(
        _proj_route_kernel,
        grid=(M // tm,),
        in_specs=[
            pl.BlockSpec((tm, K), rowblk),
            pl.BlockSpec(wblock, lambda i: lead + (0, 0), pipeline_mode=pl.Buffered(1)),
            pl.BlockSpec((tm, D), rowblk),
            pl.BlockSpec((1, D), const),
            pl.BlockSpec((D, 2 * LANES), const),
            pl.BlockSpec((1, LANES), const),
        ],
        out_specs=[
            pl.BlockSpec((tm, D), rowblk),
            pl.BlockSpec((tm * SUBLANES, LANES), rowblk),
            pl.BlockSpec((tm, LANES), rowblk),
            pl.BlockSpec((SUBLANES, tm), lambda i: (0, i)),
            pl.BlockSpec((1, LANES), const),
        ],
        out_shape=[
            jax.ShapeDtypeStruct((M, D), F32),
            jax.ShapeDtypeStruct((M * SUBLANES, LANES), U32),
            jax.ShapeDtypeStruct((M, LANES), F32),
            jax.ShapeDtypeStruct((SUBLANES, M), F32),
            jax.ShapeDtypeStruct((1, LANES), F32),
        ],
        scratch_shapes=[pltpu.VMEM((K, D), BF16), pltpu.VMEM((1, LANES), F32)],
        compiler_params=_params(("arbitrary",)),
        name="proj_route",
    )(a, w_all, x, g_ffn.reshape(1, D), w_route, b_route)


def routing_plan(route_t, counts, n_tiles_max):
    te = EXPERT_TILE
    eid = route_t[0:2].astype(jnp.int32)
    rank = route_t[2:4].astype(jnp.int32)
    cnt = counts[0, N_GROUPS:N_GROUPS + N_EXPERTS].astype(jnp.int32)
    tiles = (cnt + te - 1) // te
    tile_end = jnp.cumsum(tiles)
    offs = (tile_end - tiles) * te
    dest = rank
    for e in range(N_EXPERTS):
        dest = dest + jnp.where(eid == e, offs[e], 0)
    n_tiles = tile_end[-1]
    tid = jnp.arange(n_tiles_max, dtype=jnp.int32)
    tile_expert = jnp.sum((jnp.minimum(tid, n_tiles - 1)[:, None] >= tile_end[None, :]).astype(jnp.int32), axis=1)
    pad_start = offs + cnt
    pad_len = tiles * te - cnt
    return dest.reshape(-1), tile_expert, n_tiles.reshape(1), pad_start, pad_len


def _row(ref, r):
    start = r * SUBLANES if isinstance(r, int) else pl.multiple_of(r * SUBLANES, SUBLANES)
    return ref.at[pl.ds(start, SUBLANES), :]


def _dispatch_kernel(dest_ref, pstart_ref, plen_ref, nt_ref, xn_ref, xs_hbm, zbuf, sem, zsem, tsem, *, tm, n_rows):
    i = pl.program_id(0)
    base = i * tm
    te = zbuf.shape[0] // SUBLANES
    n_tiles_max = xs_hbm.shape[0] // zbuf.shape[0]

    def row_copy(r, slot):
        return pltpu.make_async_copy(_row(xn_ref, r), _row(xs_hbm, slot), sem)

    def zero_copy(slot):
        return pltpu.make_async_copy(_row(zbuf, 0), _row(xs_hbm, slot), zsem)

    def zero_tile_copy(tile):
        rows = zbuf.shape[0]
        return pltpu.make_async_copy(zbuf, xs_hbm.at[pl.ds(pl.multiple_of(tile * rows, rows), rows), :], tsem)

    def issue(k, _):
        for u in range(DMA_UNROLL):
            r = k * DMA_UNROLL + u
            row_copy(r, dest_ref[base + r]).start()
            row_copy(r, dest_ref[n_rows + base + r]).start()
        return 0

    lax.fori_loop(0, tm // DMA_UNROLL, issue, 0)

    @pl.when(i == 0)
    def _():
        zbuf[...] = jnp.zeros_like(zbuf)
        for e in range(N_EXPERTS):
            def zissue(r, _, e=e):
                zero_copy(pstart_ref[e] + r).start()
                return 0

            lax.fori_loop(0, plen_ref[e], zissue, 0)

        def tissue(t, _):
            zero_tile_copy(t).start()
            return 0

        lax.fori_loop(nt_ref[0], n_tiles_max, tissue, 0)
        for e in range(N_EXPERTS):
            def zwait(r, _):
                zero_copy(0).wait()
                return 0

            lax.fori_loop(0, plen_ref[e], zwait, 0)

        def twait(t, _):
            zero_tile_copy(0).wait()
            return 0

        lax.fori_loop(nt_ref[0], n_tiles_max, twait, 0)

    def wait(k, _):
        for u in range(2 * DMA_UNROLL):
            row_copy(0, 0).wait()
        return 0

    lax.fori_loop(0, tm // DMA_UNROLL, wait, 0)


def dispatch_call(xnp, dest, pad_start, pad_len, n_tiles, n_slots, tm=512):
    M = xnp.shape[0] // SUBLANES
    kern = functools.partial(_dispatch_kernel, tm=tm, n_rows=M)
    return pl.pallas_call(
        kern,
        grid_spec=pltpu.PrefetchScalarGridSpec(
            num_scalar_prefetch=4,
            grid=(M // tm,),
            in_specs=[pl.BlockSpec((tm * SUBLANES, LANES), lambda i, *_: (i, 0))],
            out_specs=pl.BlockSpec(memory_space=pl.ANY),
            scratch_shapes=[pltpu.VMEM((EXPERT_TILE * SUBLANES, LANES), U32)] + [pltpu.SemaphoreType.DMA(())] * 3,
        ),
        out_shape=jax.ShapeDtypeStruct((n_slots * SUBLANES, LANES), U32),
        compiler_params=_params(("arbitrary",)),
        name="dispatch",
    )(dest, pad_start, pad_len, n_tiles, xnp)


def _experts_kernel(te_ref, nt_ref, xs_ref, wg_ref, wu_ref, wd_ref, ys_ref, wgs, wus, wds):
    t = pl.program_id(0)
    te = xs_ref.shape[0] // SUBLANES
    valid = t < nt_ref[0]
    prev = te_ref[jnp.maximum(t - 1, 0)]
    fresh = jnp.logical_or(t == 0, te_ref[t] != prev)

    @pl.when(jnp.logical_and(valid, fresh))
    def _():
        wgs[...] = wg_ref[...].astype(BF16)
        wus[...] = wu_ref[...].astype(BF16)
        wds[...] = wd_ref[...].astype(BF16)

    @pl.when(valid)
    def _():
        los, his = _unpack_load(xs_ref, te)
        x = jnp.concatenate([p.astype(BF16) for p in los + his], axis=1)
        hg = jnp.dot(x, wgs[...], preferred_element_type=F32)
        hu = jnp.dot(x, wus[...], preferred_element_type=F32)
        h = (hg * (1.0 / (1.0 + jnp.exp(-hg)))) * hu
        _pack_store(ys_ref, jnp.dot(h.astype(BF16), wds[...], preferred_element_type=F32))

    @pl.when(jnp.logical_not(valid))
    def _():
        ys_ref[...] = jnp.zeros_like(ys_ref)


def experts_call(xs, tile_expert, n_tiles, w_gate, w_up, w_down, layer):
    D, Fd = w_gate.shape[-2:]
    te = EXPERT_TILE
    blk = (te * SUBLANES, LANES)
    n_grid = xs.shape[0] // blk[0]
    wsel = lambda t, te_ref, nt_ref: (layer, te_ref[t], 0, 0)
    return pl.pallas_call(
        _experts_kernel,
        grid_spec=pltpu.PrefetchScalarGridSpec(
            num_scalar_prefetch=2,
            grid=(n_grid,),
            in_specs=[
                pl.BlockSpec(blk, lambda t, te_ref, nt_ref: (jnp.minimum(t, nt_ref[0] - 1), 0)),
                pl.BlockSpec((None, None, D, Fd), wsel),
                pl.BlockSpec((None, None, D, Fd), wsel),
                pl.BlockSpec((None, None, Fd, D), wsel),
            ],
            out_specs=pl.BlockSpec(blk, lambda t, te_ref, nt_ref: (t, 0)),
            scratch_shapes=[pltpu.VMEM((D, Fd), BF16), pltpu.VMEM((D, Fd), BF16), pltpu.VMEM((Fd, D), BF16)],
        ),
        out_shape=jax.ShapeDtypeStruct(xs.shape, U32),
        compiler_params=_params(("arbitrary",)),
        name="experts",
    )(tile_expert, n_tiles, xs, w_gate, w_up, w_down)


def _combine_kernel(dest_ref, x1_ref, route_ref, ys_hbm, *rest, tm, n_rows, n_gain, emit_x, split_tiles):
    gain_refs = rest[:n_gain]
    n_out = (1 if emit_x else 0) + (2 if split_tiles else n_gain)
    out_refs = rest[n_gain:n_gain + n_out]
    buf0, buf1, sem = rest[n_gain + n_out:]
    i = pl.program_id(0)
    base = i * tm

    def row_copy(slot, buf, r):
        return pltpu.make_async_copy(_row(ys_hbm, slot), _row(buf, r), sem)

    def issue(k, _):
        for u in range(DMA_UNROLL):
            r = k * DMA_UNROLL + u
            row_copy(dest_ref[base + r], buf0, r).start()
            row_copy(dest_ref[n_rows + base + r], buf1, r).start()
        return 0

    lax.fori_loop(0, tm // DMA_UNROLL, issue, 0)

    def wait(k, _):
        for u in range(DMA_UNROLL):
            row_copy(0, buf0, 0).wait()
            row_copy(0, buf1, 0).wait()
        return 0

    lax.fori_loop(0, tm // DMA_UNROLL, wait, 0)

    route = route_ref[...]
    g0 = jnp.broadcast_to(route[:, 4:5], (tm, LANES))
    g1 = jnp.broadcast_to(route[:, 5:6], (tm, LANES))
    lo0, hi0 = _unpack_load(buf0, tm)
    lo1, hi1 = _unpack_load(buf1, tm)
    moe = jnp.concatenate([g0 * a + g1 * b for a, b in zip(lo0 + hi0, lo1 + hi1)], axis=1)
    x2 = x1_ref[...] + moe
    k = 0
    if emit_x:
        out_refs[0][...] = x2
        k = 1
    if n_gain:
        xh = _rms(x2)
        if split_tiles:
            y = (xh * gain_refs[0][...]).astype(out_refs[k].dtype)

            @pl.when(i < split_tiles)
            def _():
                out_refs[k][...] = y

            @pl.when(i >= split_tiles)
            def _():
                out_refs[k + 1][...] = y
        else:
            for g_ref, o_ref in zip(gain_refs, out_refs[k:]):
                o_ref[...] = (xh * g_ref[...]).astype(o_ref.dtype)


def combine_call(x1, route, ys, dest, gains, gain_dtypes, emit_x, split_rows=0, tm=256):
    M, D = x1.shape
    n_gain = len(gains)
    split_tiles = split_rows // tm
    kern = functools.partial(_combine_kernel, tm=tm, n_rows=M, n_gain=n_gain, emit_x=emit_x, split_tiles=split_tiles)
    rowblk = lambda i, d: (i, 0)
    const = lambda i, d: (0, 0)
    out_specs = [pl.BlockSpec((tm, D), rowblk)] if emit_x else []
    out_shape = [jax.ShapeDtypeStruct((M, D), F32)] if emit_x else []
    if split_tiles:
        assert n_gain == 1 and split_rows % tm == 0
        out_specs += [pl.BlockSpec((tm, D), lambda i, d: (jnp.minimum(i, split_tiles - 1), 0)),
                      pl.BlockSpec((tm, D), lambda i, d: (jnp.maximum(i - split_tiles, 0), 0))]
        out_shape += [jax.ShapeDtypeStruct((split_rows, D), gain_dtypes[0]),
                      jax.ShapeDtypeStruct((M - split_rows, D), gain_dtypes[0])]
    else:
        out_specs += [pl.BlockSpec((tm, D), rowblk) for _ in gain_dtypes]
        out_shape += [jax.ShapeDtypeStruct((M, D), dt) for dt in gain_dtypes]
    return pl.pallas_call(
        kern,
        grid_spec=pltpu.PrefetchScalarGridSpec(
            num_scalar_prefetch=1,
            grid=(M // tm,),
            in_specs=[pl.BlockSpec((tm, D), rowblk), pl.BlockSpec((tm, LANES), rowblk),
                      pl.BlockSpec(memory_space=pl.ANY)] + [pl.BlockSpec((1, D), const)] * n_gain,
            out_specs=out_specs,
            scratch_shapes=[pltpu.VMEM((tm * SUBLANES, LANES), U32), pltpu.VMEM((tm * SUBLANES, LANES), U32),
                            pltpu.SemaphoreType.DMA(())],
        ),
        out_shape=out_shape,
        compiler_params=_params(("arbitrary",)),
        name="combine",
    )(dest, x1, route, ys, *[g.reshape(1, D) for g in gains])


def _softmax_pv(s_parts, v_parts):
    m = s_parts[0].max(axis=-1, keepdims=True)
    for s in s_parts[1:]:
        m = jnp.maximum(m, s.max(axis=-1, keepdims=True))
    l = None
    o = None
    for s, v in zip(s_parts, v_parts):
        e = jnp.exp2(s - m)
        ls = jnp.sum(e, axis=-1, keepdims=True)
        ov = jnp.dot(e.astype(BF16), v, preferred_element_type=F32)
        l = ls if l is None else l + ls
        o = ov if o is None else o + ov
    return o / l


def _qk(qh, kh):
    return lax.dot_general(qh, kh, (((1,), (1,)), ((), ())), preferred_element_type=F32)


def _attn_prompt_kernel(q_ref, *rest, n_parts, n_heads, head_dim, n_blocks):
    k_refs = rest[:n_parts]
    v_refs = rest[n_parts:2 * n_parts]
    bias_ref, o_ref = rest[2 * n_parts:]
    pw = k_refs[0].shape[0]

    @pl.when(pl.program_id(0) < n_blocks)
    def _():
        for h in range(n_heads):
            hs = slice(h * head_dim, (h + 1) * head_dim)
            qh = q_ref[:, hs]
            s_parts = [_qk(qh, k_refs[p][:, hs]) + bias_ref[h, :, p * pw:(p + 1) * pw] for p in range(n_parts)]
            o_ref[:, hs] = _softmax_pv(s_parts, [v_refs[p][:, hs] for p in range(n_parts)]).astype(o_ref.dtype)

    @pl.when(pl.program_id(0) >= n_blocks)
    def _():
        o_ref[...] = jnp.zeros_like(o_ref)


def attn_prompt_call(q, kv, bias, n_rows, n_heads, head_dim):
    D = n_heads * head_dim
    tq = ATTN_GROUP * CHUNK
    n_parts = (N_BACK_CHUNKS * CHUNK) // tq + 1
    kern = functools.partial(_attn_prompt_kernel, n_parts=n_parts, n_heads=n_heads, head_dim=head_dim,
                             n_blocks=n_rows // tq)
    kspec = lambda p, col: pl.BlockSpec(
        (tq, D), lambda b, p=p, col=col: (jnp.maximum(b - (n_parts - 1) + p, 0), col))
    return pl.pallas_call(
        kern,
        grid=(q.shape[0] // tq,),
        in_specs=[pl.BlockSpec((tq, D), lambda b: (b, 0))]
        + [kspec(p, 0) for p in range(n_parts)] + [kspec(p, 1) for p in range(n_parts)]
        + [pl.BlockSpec((None,) + bias.shape[1:], lambda b: (jnp.minimum(b, n_parts - 1), 0, 0, 0),
                        pipeline_mode=pl.Buffered(1))],
        out_specs=pl.BlockSpec((tq, D), lambda b: (b, 0)),
        out_shape=jax.ShapeDtypeStruct((q.shape[0], D), BF16),
        compiler_params=_params(("arbitrary",)),
        name="attn_prompt",
    )(q, *([kv] * (2 * n_parts)), bias)


def _attn_step_kernel(q_ref, ck_ref, cv_ref, kn_ref, vn_ref, bias_ref, o_in_ref, o_ref, *, n_heads, head_dim):
    del o_in_ref
    n_cache = ck_ref.shape[0] // n_heads
    for h in range(n_heads):
        hs = slice(h * head_dim, (h + 1) * head_dim)
        qh = q_ref[:, hs]
        kc = ck_ref[pl.ds(h, n_cache, stride=n_heads), :].astype(BF16)
        vc = cv_ref[pl.ds(h, n_cache, stride=n_heads), :].astype(BF16)
        s_parts = [_qk(qh, kc) + bias_ref[h, :, :n_cache], _qk(qh, kn_ref[:, hs]) + bias_ref[h, :, n_cache:]]
        o_ref[:, hs] = _softmax_pv(s_parts, [vc, vn_ref[:, hs]]).astype(o_ref.dtype)


def attn_step_call(q, cache_k, cache_v, kv, bias, o_full, row0, n_heads, head_dim):
    B, n_cache = cache_k.shape[:2]
    D = n_heads * head_dim
    S = CHUNK
    blk0 = row0 // S
    ck = cache_k.reshape(B, n_cache * n_heads, head_dim)
    cv = cache_v.reshape(B, n_cache * n_heads, head_dim)
    kern = functools.partial(_attn_step_kernel, n_heads=n_heads, head_dim=head_dim)
    newspec = lambda col: pl.BlockSpec((S, D), lambda b, col=col: (blk0 + b, col))
    return pl.pallas_call(
        kern,
        grid=(B,),
        in_specs=[newspec(0),
                  pl.BlockSpec((None, n_cache * n_heads, head_dim), lambda b: (b, 0, 0)),
                  pl.BlockSpec((None, n_cache * n_heads, head_dim), lambda b: (b, 0, 0)),
                  newspec(0), newspec(1),
                  pl.BlockSpec(bias.shape, lambda b: (0, 0, 0)),
                  pl.BlockSpec(memory_space=pl.ANY)],
        out_specs=newspec(0),
        out_shape=jax.ShapeDtypeStruct(o_full.shape, o_full.dtype),
        input_output_aliases={6: 0},
        compiler_params=_params(("arbitrary",)),
        name="attn_step",
    )(q, ck, cv, kv, kv, bias, o_full)


def band_bias(table):
    H = table.shape[0]
    tq = ATTN_GROUP * CHUNK
    nk = tq + N_BACK_CHUNKS * CHUNK
    n_parts = nk // tq
    P = tq + nk
    d = jnp.concatenate([-jnp.arange(nk), jnp.zeros((1,), jnp.int32), jnp.arange(tq - 1, 0, -1)])
    w = table[:, jnp.clip(N_BACK_CHUNKS * CHUNK + d, -REL_CLIP, REL_CLIP) + REL_CLIP].astype(F32) * LOG2E
    b = jnp.tile(w, (1, tq))[:, :tq * (P - 1)].reshape(H, tq, P - 1)[:, :, :nk]
    i = jnp.arange(tq)[:, None]
    j = jnp.arange(nk)[None, :]
    qc, jc = i // CHUNK, j // CHUNK
    band = (jc >= qc) & (jc <= qc + N_BACK_CHUNKS)
    v = jnp.arange(n_parts)[:, None, None]
    ok = band[None] & (j[None] >= (n_parts - 1 - v) * tq)
    return jnp.where(ok[:, None], b[None], -jnp.inf)


def kernel(x_prompt, x_sample, state_conv, cache_k, cache_v, g_mix, g_ffn, w_in_a, conv_a, w_out_a, g_kv, w_kv,
           w_q_b, w_o_b, rel_bias_b, w_group, b_group, w_router, b_router, w_gate, w_up, w_down, g_final):
    Bp, Tp, D = x_prompt.shape
    Bs, Ss, _ = x_sample.shape
    n_a = w_in_a.shape[0]
    depth = g_mix.shape[0]
    n_heads, head_dim = cache_k.shape[2], cache_k.shape[3]
    n_cache = cache_k.shape[1]
    assert Bp == 1 and Ss == CHUNK
    n_p = Bp * Tp
    n_s = Bs * Ss
    M = n_p + n_s
    n_tiles_max = (2 * M) // EXPERT_TILE + N_EXPERTS
    n_slots = n_tiles_max * EXPERT_TILE

    x = jnp.concatenate([x_prompt.reshape(n_p, D), x_sample.reshape(n_s, D)], axis=0)

    def moe(x1, xnp, route, route_t, counts, layer, gains, gain_dtypes, emit_x, split_rows=0):
        dest, tile_expert, n_tiles, pad_start, pad_len = routing_plan(route_t, counts, n_tiles_max)
        xs = dispatch_call(xnp, dest, pad_start, pad_len, n_tiles, n_slots)
        ys = experts_call(xs, tile_expert, n_tiles, w_gate, w_up, w_down, layer)
        return combine_call(x1, route, ys, dest, gains, gain_dtypes, emit_x, split_rows)

    def router_weights(layer):
        pad = LANES - N_GROUPS - N_EXPERTS
        w = jnp.concatenate([w_group[layer], w_router[layer], jnp.zeros((D, pad), F32)], axis=1)
        w_hi = w.astype(BF16)
        w_lo = (w - w_hi.astype(F32)).astype(BF16)
        w = jnp.concatenate([w_hi, w_lo], axis=1)
        b = jnp.concatenate([b_group[layer], b_router[layer], jnp.zeros((pad,), F32)]).reshape(1, LANES)
        return w, b

    xn = rmsnorm_call(x, g_mix[0], BF16)
    conv_p, conv_s = [], []
    for l in range(n_a):
        st = state_conv[l]
        z = jnp.zeros((Bs, Ss - 1, D), F32)
        inj1 = jnp.concatenate([st[:, 1:2], z], axis=1).reshape(n_s, D)
        inj2 = jnp.concatenate([st, z[:, 1:]], axis=1).reshape(n_s, D)
        g, cp, cs = conv_in_call(xn, w_in_a, conv_a, l, inj1, inj2, n_p, Bs, Ss)
        conv_p.append(cp.reshape(Bp, 2, D))
        conv_s.append(cs)
        wr, br = router_weights(l)
        routed = proj_route_call(g, w_out_a, (l,), x, g_ffn[l], wr, br)
        if l + 1 < n_a:
            x, xn = moe(*routed, l, [g_mix[l + 1]], [BF16], True)
        else:
            x, xn_kv, xn = moe(*routed, l, [g_kv, g_mix[l + 1]], [BF16, BF16], True)

    keep_p = min(N_BACK_CHUNKS * CHUNK, Tp)
    mm_tile = 512
    assert keep_p == mm_tile and n_s == mm_tile
    kv_bf, kv_tail = matmul_call(xn_kv, w_kv, (), [BF16, F32], tail_tiles=(0, 2), tm=mm_tile, name="kv_proj")

    y_prompt = y_sample = None
    for j in range(depth - n_a):
        l = n_a + j
        (q,) = matmul_call(xn, w_q_b, (j,), [BF16], out_scale=head_dim ** -0.5 * LOG2E, name="q_proj")
        bias = band_bias(rel_bias_b[j])
        o = attn_prompt_call(q, kv_bf, bias, n_p, n_heads, head_dim)
        o = attn_step_call(q, cache_k, cache_v, kv_bf, bias[-1, :, :Ss, :n_cache + Ss], o, n_p, n_heads, head_dim)
        wr, br = router_weights(l)
        routed = proj_route_call(o, w_o_b, (j,), x, g_ffn[l], wr, br)
        if l + 1 < depth:
            x, xn = moe(*routed, l, [g_mix[l + 1]], [BF16], True)
        else:
            y_prompt, y_sample = moe(*routed, l, [g_final], [F32], False, split_rows=n_p)

    new_conv_prompt = jnp.stack(conv_p, axis=0)
    new_conv_sample = jnp.stack(conv_s, axis=0)
    k_tail = kv_tail[:, :D].reshape(2, mm_tile, n_heads, head_dim)
    v_tail = kv_tail[:, D:].reshape(2, mm_tile, n_heads, head_dim)
    new_k_prompt = k_tail[0].reshape(Bp, keep_p, n_heads, head_dim)
    new_v_prompt = v_tail[0].reshape(Bp, keep_p, n_heads, head_dim)
    new_k_sample = jnp.concatenate([cache_k[:, Ss:], k_tail[1].reshape(Bs, Ss, n_heads, head_dim)], axis=1)
    new_v_sample = jnp.concatenate([cache_v[:, Ss:], v_tail[1].reshape(Bs, Ss, n_heads, head_dim)], axis=1)
    return (y_prompt.reshape(Bp, Tp, D), y_sample.reshape(Bs, Ss, D), new_conv_prompt, new_k_prompt, new_v_prompt,
            new_conv_sample, new_k_sample, new_v_sample)
```

```python
import functools
import math

import jax
import jax.numpy as jnp
from jax import lax
from jax.experimental import pallas as pl
from jax.experimental.pallas import tpu as pltpu

F32 = jnp.float32
BF16 = jnp.bfloat16

CHUNK = 64
N_BACK_CHUNKS = 8
REL_CLIP = 128
N_GROUPS = 4
EXPERTS_PER_GROUP = 4
N_EXPERTS = N_GROUPS * EXPERTS_PER_GROUP
EPS = 1e-6

LANES = 128
SUBLANES = 8
EXPERT_TILE = 256
ATTN_GROUP = 4
VMEM_LIMIT = 56 * 1024 * 1024
LOG2E = math.log2(math.e)


def _params(sem, **kw):
    return pltpu.CompilerParams(dimension_semantics=sem, vmem_limit_bytes=VMEM_LIMIT, **kw)


def _rms(x):
    return x * lax.rsqrt(jnp.mean(x * x, axis=-1, keepdims=True) + EPS)


def _norm_in_kernel(xa_ref, xb_ref, g_ref, x_ref, xn_ref, *, a_tiles):
    def emit(src):
        x = src[...]
        x_ref[...] = x
        xn_ref[...] = (_rms(x) * g_ref[...]).astype(xn_ref.dtype)

    @pl.when(pl.program_id(0) < a_tiles)
    def _():
        emit(xa_ref)

    @pl.when(pl.program_id(0) >= a_tiles)
    def _():
        emit(xb_ref)


def norm_in_call(xa, xb, g, tm=512):
    (na, D), nb = xa.shape, xb.shape[0]
    a_tiles = na // tm
    assert na % tm == 0 and nb % tm == 0
    M = na + nb
    return pl.pallas_call(
        functools.partial(_norm_in_kernel, a_tiles=a_tiles),
        grid=(M // tm,),
        in_specs=[pl.BlockSpec((tm, D), lambda i: (jnp.minimum(i, a_tiles - 1), 0)),
                  pl.BlockSpec((tm, D), lambda i: (jnp.maximum(i - a_tiles, 0), 0)),
                  pl.BlockSpec((1, D), lambda i: (0, 0))],
        out_specs=[pl.BlockSpec((tm, D), lambda i: (i, 0)), pl.BlockSpec((tm, D), lambda i: (i, 0))],
        out_shape=[jax.ShapeDtypeStruct((M, D), F32), jax.ShapeDtypeStruct((M, D), BF16)],
        compiler_params=_params(("arbitrary",)),
        name="norm_in",
    )(xa, xb, g.reshape(1, D))


def _conv_in_kernel(xn_ref, wb_ref, wc_ref, wh_ref, cw_ref, inj1_ref, inj2_ref,
                    g_ref, cp_ref, cs_ref, wbs, wcs, whs, carry, u_s, *, n_prompt_tiles, n_seq, seq_len):
    i = pl.program_id(1)
    tm = u_s.shape[0]

    @pl.when(i == 0)
    def _():
        wbs[...] = wb_ref[...].astype(BF16)
        wcs[...] = wc_ref[...].astype(BF16)
        whs[...] = wh_ref[...].astype(BF16)
        carry[...] = jnp.zeros_like(carry)

    xn = xn_ref[...]
    b = jnp.dot(xn, wbs[...], preferred_element_type=F32)
    c = jnp.dot(xn, wcs[...], preferred_element_type=F32)
    h = jnp.dot(xn, whs[...], preferred_element_type=F32)
    u = c * h
    u_s[...] = u
    is_sample = i >= n_prompt_tiles
    row = lax.broadcasted_iota(jnp.int32, u.shape, 0)
    rowm = row & jnp.where(is_sample, seq_len - 1, 0x7FFFFFFF)
    c0 = carry[0:1, :]
    c1 = carry[1:2, :]
    e1 = jnp.where(is_sample, inj1_ref[...], c1)
    e2 = jnp.where(is_sample, inj2_ref[...], jnp.where(row == 0, c0, c1))
    up1 = jnp.where(rowm == 0, e1, pltpu.roll(u, 1, 0))
    up2 = jnp.where(rowm < 2, e2, pltpu.roll(u, 2, 0))
    cw = cw_ref[...]
    conv = cw[0:1, :] * up2 + cw[1:2, :] * up1 + cw[2:3, :] * u
    g_ref[...] = (b * conv).astype(g_ref.dtype)
    carry[...] = u_s[tm - 2:tm, :]

    @pl.when(i == n_prompt_tiles - 1)
    def _():
        cp_ref[...] = u_s[tm - 2:tm, :]

    @pl.when(i == n_prompt_tiles)
    def _():
        for s in range(n_seq):
            cs_ref[s] = u_s[(s + 1) * seq_len - 2:(s + 1) * seq_len, :]


def conv_in_call(xn, w_in_all, conv_all, layer, inj1, inj2, n_prompt_rows, n_seq, seq_len, tn=512):
    M, D = xn.shape
    tm = n_seq * seq_len
    assert M == n_prompt_rows + tm and n_prompt_rows % tm == 0 and D % tn == 0
    nj, ni = D // tn, M // tm
    kern = functools.partial(_conv_in_kernel, n_prompt_tiles=n_prompt_rows // tm, n_seq=n_seq, seq_len=seq_len)
    wspec = lambda k: pl.BlockSpec((None, D, tn), lambda j, i, k=k: (layer, 0, j + k * nj))
    return pl.pallas_call(
        kern,
        grid=(nj, ni),
        in_specs=[
            pl.BlockSpec((tm, D), lambda j, i: (i, 0)),
            wspec(0), wspec(1), wspec(2),
            pl.BlockSpec((None, 3, tn), lambda j, i: (layer, 0, j)),
            pl.BlockSpec((tm, tn), lambda j, i: (0, j)),
            pl.BlockSpec((tm, tn), lambda j, i: (0, j)),
        ],
        out_specs=[
            pl.BlockSpec((tm, tn), lambda j, i: (i, j)),
            pl.BlockSpec((2, tn), lambda j, i: (0, j)),
            pl.BlockSpec((n_seq, 2, tn), lambda j, i: (0, 0, j)),
        ],
        out_shape=[
            jax.ShapeDtypeStruct((M, D), BF16),
            jax.ShapeDtypeStruct((2, D), F32),
            jax.ShapeDtypeStruct((n_seq, 2, D), F32),
        ],
        scratch_shapes=[pltpu.VMEM((D, tn), BF16)] * 3 + [pltpu.VMEM((2, tn), F32), pltpu.VMEM((tm, tn), F32)],
        compiler_params=_params(("arbitrary", "arbitrary")),
        name="conv_in",
    )(xn, w_in_all, w_in_all, w_in_all, conv_all, inj1, inj2)


def _matmul_kernel(x_ref, w_ref, *rest, out_scale):
    out_refs, ws = rest[:-1], rest[-1]

    @pl.when(pl.program_id(1) == 0)
    def _():
        ws[...] = w_ref[...].astype(BF16)

    y = jnp.dot(x_ref[...], ws[...], preferred_element_type=F32)
    if out_scale is not None:
        y = y * out_scale
    for o in out_refs:
        o[...] = y.astype(o.dtype)


def matmul_call(x, w, w_index, out_dtypes, tail_tiles=(), out_scale=None, tm=512, tn=512, name="matmul"):
    M, K = x.shape
    N = w.shape[-1]
    ni = M // tm
    lead = tuple(w_index)
    wblock = (None,) * len(lead) + (K, tn)
    tails = tuple(tail_tiles) + (0,) * (len(out_dtypes) - len(tail_tiles))
    out_specs, out_shape = [], []
    for dt, n in zip(out_dtypes, tails):
        if n:
            out_specs.append(pl.BlockSpec((tm, tn), lambda j, i, n=n: (jnp.maximum(i - (ni - n), 0), j)))
            out_shape.append(jax.ShapeDtypeStruct((n * tm, N), dt))
        else:
            out_specs.append(pl.BlockSpec((tm, tn), lambda j, i: (i, j)))
            out_shape.append(jax.ShapeDtypeStruct((M, N), dt))
    return pl.pallas_call(
        functools.partial(_matmul_kernel, out_scale=out_scale),
        grid=(N // tn, ni),
        in_specs=[pl.BlockSpec((tm, K), lambda j, i: (i, 0)),
                  pl.BlockSpec(wblock, lambda j, i: lead + (0, j))],
        out_specs=out_specs,
        out_shape=out_shape,
        scratch_shapes=[pltpu.VMEM((K, tn), BF16)],
        compiler_params=_params(("arbitrary", "arbitrary")),
        name=name,
    )(x, w)


def _route(logits, carry):
    tm = logits.shape[0]
    lane = lax.broadcasted_iota(jnp.int32, logits.shape, 1).astype(F32)
    neg = -jnp.inf
    big = float(LANES)
    gl = jnp.where(lane < N_GROUPS, logits, neg)
    gmax = jnp.max(gl, axis=-1, keepdims=True)
    gidx = jnp.min(jnp.where(gl == gmax, lane, big), axis=-1, keepdims=True)
    gp = 1.0 / jnp.sum(jnp.exp(gl - gmax), axis=-1, keepdims=True)
    lo = N_GROUPS + gidx * EXPERTS_PER_GROUP
    el = jnp.where(lane >= lo, jnp.where(lane < lo + EXPERTS_PER_GROUP, logits, neg), neg)
    t1 = jnp.max(el, axis=-1, keepdims=True)
    i1 = jnp.min(jnp.where(el == t1, lane, big), axis=-1, keepdims=True)
    el2 = jnp.where(lane == i1, neg, el)
    t2 = jnp.max(el2, axis=-1, keepdims=True)
    i2 = jnp.min(jnp.where(el2 == t2, lane, big), axis=-1, keepdims=True)
    ex = jnp.exp(t2 - t1)
    den = 1.0 + ex
    w0 = (1.0 / den) * gp
    w1 = (ex / den) * gp
    hit1 = lane == i1
    hit2 = lane == i2
    cnt = jnp.where(hit1, 1.0, jnp.where(hit2, 1.0, 0.0))
    r_i = lax.broadcasted_iota(jnp.int32, (tm, tm), 0)
    c_i = lax.broadcasted_iota(jnp.int32, (tm, tm), 1)
    tri = jnp.where(c_i < r_i, 1.0, 0.0).astype(BF16)
    before = jnp.dot(tri, cnt.astype(BF16), preferred_element_type=F32) + carry[...]
    rank0 = jnp.sum(jnp.where(hit1, before, 0.0), axis=-1, keepdims=True)
    rank1 = jnp.sum(jnp.where(hit2, before, 0.0), axis=-1, keepdims=True)
    carry[...] = carry[...] + jnp.sum(cnt, axis=0, keepdims=True)
    vals = (i1 - N_GROUPS, i2 - N_GROUPS, rank0, rank1, w0, w1)
    route = jnp.zeros_like(logits)
    for k, v in enumerate(vals):
        route = jnp.where(lane == float(k), v, route)
    return route


def _proj_route_kernel(a_ref, w_ref, x_ref, gf_ref, wr_ref, br_ref,
                       x1_ref, xn2_ref, route_ref, routet_ref, cnt_ref, ws, carry):
    K = ws.shape[0]

    @pl.when(pl.program_id(0) == 0)
    def _():
        step = 256

        def body(k, _):
            r = pl.multiple_of(k * step, step)
            ws[pl.ds(r, step), :] = w_ref[pl.ds(r, step), :].astype(BF16)
            return 0

        lax.fori_loop(0, K // step, body, 0)
        carry[...] = jnp.zeros_like(carry)

    y = jnp.dot(a_ref[...], ws[...], preferred_element_type=F32)
    x1 = x_ref[...] + y
    x1_ref[...] = x1
    xn = _rms(x1) * gf_ref[...]
    xn2_ref[...] = xn
    xh = xn.astype(BF16)
    xl = (xn - xh.astype(F32)).astype(BF16)
    p1 = jnp.dot(xh, wr_ref[...], preferred_element_type=F32)
    p2 = jnp.dot(xl, wr_ref[:, :LANES], preferred_element_type=F32)
    logits = p1[:, :LANES] + (p1[:, LANES:] + p2) + br_ref[...]
    route = _route(logits, carry)
    route_ref[...] = route
    routet_ref[...] = route.T[:SUBLANES, :]
    cnt_ref[...] = carry[...]


def proj_route_call(a, w_all, w_index, x, g_ffn, w_route, b_route, tm=256):
    M, K = a.shape
    D = x.shape[1]
    lead = tuple(w_index)
    wblock = (None,) * len(lead) + (K, D)
    const = lambda i: (0, 0)
    rowblk = lambda i: (i, 0)
    return pl.pallas_call(
        _proj_route_kernel,
        grid=(M // tm,),
        in_specs=[
            pl.BlockSpec((tm, K), rowblk),
            pl.BlockSpec(wblock, lambda i: lead + (0, 0), pipeline_mode=pl.Buffered(1)),
            pl.BlockSpec((tm, D), rowblk),
            pl.BlockSpec((1, D), const),
            pl.BlockSpec((D, 2 * LANES), const),
            pl.BlockSpec((1, LANES), const),
        ],
        out_specs=[
            pl.BlockSpec((tm, D), rowblk),
            pl.BlockSpec((tm, D), rowblk),
            pl.BlockSpec((tm, LANES), rowblk),
            pl.BlockSpec((SUBLANES, tm), lambda i: (0, i)),
            pl.BlockSpec((1, LANES), const),
        ],
        out_shape=[
            jax.ShapeDtypeStruct((M, D), F32),
            jax.ShapeDtypeStruct((M, D), F32),
            jax.ShapeDtypeStruct((M, LANES), F32),
            jax.ShapeDtypeStruct((SUBLANES, M), F32),
            jax.ShapeDtypeStruct((1, LANES), F32),
        ],
        scratch_shapes=[pltpu.VMEM((K, D), BF16), pltpu.VMEM((1, LANES), F32)],
        compiler_params=_params(("arbitrary",)),
        name="proj_route",
    )(a, w_all, x, g_ffn.reshape(1, D), w_route, b_route)


def routing_plan(route_t, counts, n_tiles_max):
    te = EXPERT_TILE
    eid = route_t[0:2].astype(jnp.int32)
    rank = route_t[2:4].astype(jnp.int32)
    cnt = counts[0, N_GROUPS:N_GROUPS + N_EXPERTS].astype(jnp.int32)
    tiles = (cnt + te - 1) // te
    tile_end = jnp.cumsum(tiles)
    offs = (tile_end - tiles) * te
    dest = rank
    for e in range(N_EXPERTS):
        dest = dest + jnp.where(eid == e, offs[e], 0)
    n_tiles = tile_end[-1]
    tid = jnp.arange(n_tiles_max, dtype=jnp.int32)
    tile_expert = jnp.sum((jnp.minimum(tid, n_tiles - 1)[:, None] >= tile_end[None, :]).astype(jnp.int32), axis=1)
    pad_start = offs + cnt
    pad_len = tiles * te - cnt
    return dest.reshape(-1), tile_expert, n_tiles.reshape(1), pad_start, pad_len


def _slab_row(ref, r):
    return ref.at[r >> 3, pl.ds(r & (SUBLANES - 1), 1), :]


def _dispatch_kernel(dest_ref, pstart_ref, plen_ref, nt_ref, xn_ref, xs_hbm, zbuf, sem, zsem, tsem, *, tm, n_rows):
    i = pl.program_id(0)
    base = i * tm
    tile_slabs = zbuf.shape[0]
    n_tiles_max = xs_hbm.shape[0] // tile_slabs

    def row_copy(k, u, slot):
        return pltpu.make_async_copy(xn_ref.at[k, pl.ds(u, 1), :], _slab_row(xs_hbm, slot), sem)

    def zero_copy(slot):
        return pltpu.make_async_copy(zbuf.at[0, pl.ds(0, 1), :], _slab_row(xs_hbm, slot), zsem)

    def zero_tile_copy(tile):
        return pltpu.make_async_copy(zbuf, xs_hbm.at[pl.ds(tile * tile_slabs, tile_slabs)], tsem)

    def issue(k, _):
        for u in range(SUBLANES):
            r = k * SUBLANES + u
            row_copy(k, u, dest_ref[base + r]).start(priority=0)
            row_copy(k, u, dest_ref[n_rows + base + r]).start(priority=1)
        return 0

    lax.fori_loop(0, tm // SUBLANES, issue, 0)

    @pl.when(i == 0)
    def _():
        zbuf[...] = jnp.zeros_like(zbuf)
        for e in range(N_EXPERTS):
            def zissue(r, _, e=e):
                zero_copy(pstart_ref[e] + r).start()
                return 0

            lax.fori_loop(0, plen_ref[e], zissue, 0)

        def tissue(t, _):
            zero_tile_copy(t).start()
            return 0

        lax.fori_loop(nt_ref[0], n_tiles_max, tissue, 0)
        for e in range(N_EXPERTS):
            def zwait(r, _):
                zero_copy(0).wait()
                return 0

            lax.fori_loop(0, plen_ref[e], zwait, 0)

        def twait(t, _):
            zero_tile_copy(0).wait()
            return 0

        lax.fori_loop(nt_ref[0], n_tiles_max, twait, 0)

    def wait(k, _):
        for u in range(2 * SUBLANES):
            row_copy(0, 0, 0).wait()
        return 0

    lax.fori_loop(0, tm // SUBLANES, wait, 0)


def dispatch_call(xn, dest, pad_start, pad_len, n_tiles, n_slots, tm=512):
    M, D = xn.shape
    kern = functools.partial(_dispatch_kernel, tm=tm, n_rows=M)
    xs = pl.pallas_call(
        kern,
        grid_spec=pltpu.PrefetchScalarGridSpec(
            num_scalar_prefetch=4,
            grid=(M // tm,),
            in_specs=[pl.BlockSpec((tm // SUBLANES, SUBLANES, D), lambda i, *_: (i, 0, 0))],
            out_specs=pl.BlockSpec(memory_space=pl.ANY),
            scratch_shapes=[pltpu.VMEM((EXPERT_TILE // SUBLANES, SUBLANES, D), xn.dtype)]
            + [pltpu.SemaphoreType.DMA(())] * 3,
        ),
        out_shape=jax.ShapeDtypeStruct((n_slots // SUBLANES, SUBLANES, D), xn.dtype),
        compiler_params=_params(("arbitrary",)),
        name="dispatch",
    )(dest, pad_start, pad_len, n_tiles, xn.reshape(M // SUBLANES, SUBLANES, D))
    return xs.reshape(n_slots, D)


def _experts_kernel(te_ref, nt_ref, xs_ref, wg_ref, wu_ref, wd_ref, ys_ref, wgs, wus, wds):
    t = pl.program_id(0)
    valid = t < nt_ref[0]
    prev = te_ref[jnp.maximum(t - 1, 0)]
    fresh = jnp.logical_or(t == 0, te_ref[t] != prev)

    @pl.when(jnp.logical_and(valid, fresh))
    def _():
        wgs[...] = wg_ref[...].astype(BF16)
        wus[...] = wu_ref[...].astype(BF16)
        wds[...] = wd_ref[...].astype(BF16)

    @pl.when(valid)
    def _():
        x = xs_ref[...].astype(BF16)
        hg = jnp.dot(x, wgs[...], preferred_element_type=F32)
        hu = jnp.dot(x, wus[...], preferred_element_type=F32)
        h = (hg * (1.0 / (1.0 + jnp.exp(-hg)))) * hu
        ys_ref[...] = jnp.dot(h.astype(BF16), wds[...], preferred_element_type=F32)

    @pl.when(jnp.logical_not(valid))
    def _():
        ys_ref[...] = jnp.zeros_like(ys_ref)


def experts_call(xs, tile_expert, n_tiles, w_gate, w_up, w_down, layer):
    S, D = xs.shape
    Fd = w_gate.shape[-1]
    te = EXPERT_TILE
    wsel = lambda t, te_ref, nt_ref: (layer, te_ref[t], 0, 0)
    return pl.pallas_call(
        _experts_kernel,
        grid_spec=pltpu.PrefetchScalarGridSpec(
            num_scalar_prefetch=2,
            grid=(S // te,),
            in_specs=[
                pl.BlockSpec((te, D), lambda t, te_ref, nt_ref: (jnp.minimum(t, nt_ref[0] - 1), 0)),
                pl.BlockSpec((None, None, D, Fd), wsel),
                pl.BlockSpec((None, None, D, Fd), wsel),
                pl.BlockSpec((None, None, Fd, D), wsel),
            ],
            out_specs=pl.BlockSpec((te, D), lambda t, te_ref, nt_ref: (t, 0)),
            scratch_shapes=[pltpu.VMEM((D, Fd), BF16), pltpu.VMEM((D, Fd), BF16), pltpu.VMEM((Fd, D), BF16)],
        ),
        out_shape=jax.ShapeDtypeStruct((S, D), F32),
        compiler_params=_params(("arbitrary",)),
        name="experts",
    )(tile_expert, n_tiles, xs, w_gate, w_up, w_down)


def _combine_kernel(dest_ref, x1_ref, route_ref, ys_hbm, *rest, tm, n_rows, n_gain, emit_x, split_tiles):
    gain_refs = rest[:n_gain]
    n_out = (1 if emit_x else 0) + (2 if split_tiles else n_gain)
    out_refs = rest[n_gain:n_gain + n_out]
    buf0, buf1, sem = rest[n_gain + n_out:]
    i = pl.program_id(0)
    base = i * tm
    D = x1_ref.shape[1]

    def row_copy(slot, buf, k, u):
        return pltpu.make_async_copy(_slab_row(ys_hbm, slot), buf.at[k, pl.ds(u, 1), :], sem)

    def issue(k, _):
        for u in range(SUBLANES):
            r = k * SUBLANES + u
            row_copy(dest_ref[base + r], buf0, k, u).start(priority=0)
            row_copy(dest_ref[n_rows + base + r], buf1, k, u).start(priority=1)
        return 0

    lax.fori_loop(0, tm // SUBLANES, issue, 0)

    def wait(k, _):
        for u in range(SUBLANES):
            row_copy(0, buf0, 0, 0).wait()
            row_copy(0, buf1, 0, 0).wait()
        return 0

    lax.fori_loop(0, tm // SUBLANES, wait, 0)

    route = route_ref[...]
    y0 = buf0[...].reshape(tm, D)
    y1 = buf1[...].reshape(tm, D)
    x2 = x1_ref[...] + (route[:, 4:5] * y0 + route[:, 5:6] * y1)
    k = 0
    if emit_x:
        out_refs[0][...] = x2
        k = 1
    if n_gain:
        xh = _rms(x2)
        if split_tiles:
            y = (xh * gain_refs[0][...]).astype(out_refs[k].dtype)

            @pl.when(i < split_tiles)
            def _():
                out_refs[k][...] = y

            @pl.when(i >= split_tiles)
            def _():
                out_refs[k + 1][...] = y
        else:
            for g_ref, o_ref in zip(gain_refs, out_refs[k:]):
                o_ref[...] = (xh * g_ref[...]).astype(o_ref.dtype)


def combine_call(x1, route, ys, dest, gains, gain_dtypes, emit_x, split_rows=0, tm=256):
    M, D = x1.shape
    n_gain = len(gains)
    split_tiles = split_rows // tm
    kern = functools.partial(_combine_kernel, tm=tm, n_rows=M, n_gain=n_gain, emit_x=emit_x, split_tiles=split_tiles)
    rowblk = lambda i, d: (i, 0)
    const = lambda i, d: (0, 0)
    out_specs = [pl.BlockSpec((tm, D), rowblk)] if emit_x else []
    out_shape = [jax.ShapeDtypeStruct((M, D), F32)] if emit_x else []
    if split_tiles:
        assert n_gain == 1 and split_rows % tm == 0
        out_specs += [pl.BlockSpec((tm, D), lambda i, d: (jnp.minimum(i, split_tiles - 1), 0)),
                      pl.BlockSpec((tm, D), lambda i, d: (jnp.maximum(i - split_tiles, 0), 0))]
        out_shape += [jax.ShapeDtypeStruct((split_rows, D), gain_dtypes[0]),
                      jax.ShapeDtypeStruct((M - split_rows, D), gain_dtypes[0])]
    else:
        out_specs += [pl.BlockSpec((tm, D), rowblk) for _ in gain_dtypes]
        out_shape += [jax.ShapeDtypeStruct((M, D), dt) for dt in gain_dtypes]
    slabs = (tm // SUBLANES, SUBLANES, D)
    return pl.pallas_call(
        kern,
        grid_spec=pltpu.PrefetchScalarGridSpec(
            num_scalar_prefetch=1,
            grid=(M // tm,),
            in_specs=[pl.BlockSpec((tm, D), rowblk), pl.BlockSpec((tm, LANES), rowblk),
                      pl.BlockSpec(memory_space=pl.ANY)] + [pl.BlockSpec((1, D), const)] * n_gain,
            out_specs=out_specs,
            scratch_shapes=[pltpu.VMEM(slabs, F32), pltpu.VMEM(slabs, F32), pltpu.SemaphoreType.DMA(())],
        ),
        out_shape=out_shape,
        compiler_params=_params(("arbitrary",)),
        name="combine",
    )(dest, x1, route, ys.reshape(ys.shape[0] // SUBLANES, SUBLANES, D), *[g.reshape(1, D) for g in gains])


def _softmax_pv(s_parts, v_parts):
    m = s_parts[0].max(axis=-1, keepdims=True)
    for s in s_parts[1:]:
        m = jnp.maximum(m, s.max(axis=-1, keepdims=True))
    l = None
    o = None
    for s, v in zip(s_parts, v_parts):
        e = jnp.exp2(s - m)
        ls = jnp.sum(e, axis=-1, keepdims=True)
        ov = jnp.dot(e.astype(BF16), v, preferred_element_type=F32)
        l = ls if l is None else l + ls
        o = ov if o is None else o + ov
    return o / l


def _qk(qh, kh):
    return lax.dot_general(qh, kh, (((1,), (1,)), ((), ())), preferred_element_type=F32)


def _attn_prompt_kernel(q_ref, *rest, n_parts, n_heads, head_dim, n_blocks):
    k_refs = rest[:n_parts]
    v_refs = rest[n_parts:2 * n_parts]
    bias_ref, o_ref = rest[2 * n_parts:]
    pw = k_refs[0].shape[0]

    @pl.when(pl.program_id(0) < n_blocks)
    def _():
        for h in range(n_heads):
            hs = slice(h * head_dim, (h + 1) * head_dim)
            qh = q_ref[:, hs]
            s_parts = [_qk(qh, k_refs[p][:, hs]) + bias_ref[h, :, p * pw:(p + 1) * pw] for p in range(n_parts)]
            o_ref[:, hs] = _softmax_pv(s_parts, [v_refs[p][:, hs] for p in range(n_parts)]).astype(o_ref.dtype)

    @pl.when(pl.program_id(0) >= n_blocks)
    def _():
        o_ref[...] = jnp.zeros_like(o_ref)


def attn_prompt_call(q, kv, bias, n_rows, n_heads, head_dim):
    D = n_heads * head_dim
    tq = ATTN_GROUP * CHUNK
    n_parts = (N_BACK_CHUNKS * CHUNK) // tq + 1
    kern = functools.partial(_attn_prompt_kernel, n_parts=n_parts, n_heads=n_heads, head_dim=head_dim,
                             n_blocks=n_rows // tq)
    kspec = lambda p, col: pl.BlockSpec(
        (tq, D), lambda b, p=p, col=col: (jnp.maximum(b - (n_parts - 1) + p, 0), col))
    return pl.pallas_call(
        kern,
        grid=(q.shape[0] // tq,),
        in_specs=[pl.BlockSpec((tq, D), lambda b: (b, 0))]
        + [kspec(p, 0) for p in range(n_parts)] + [kspec(p, 1) for p in range(n_parts)]
        + [pl.BlockSpec((None,) + bias.shape[1:], lambda b: (jnp.minimum(b, n_parts - 1), 0, 0, 0),
                        pipeline_mode=pl.Buffered(1))],
        out_specs=pl.BlockSpec((tq, D), lambda b: (b, 0)),
        out_shape=jax.ShapeDtypeStruct((q.shape[0], D), BF16),
        compiler_params=_params(("arbitrary",)),
        name="attn_prompt",
    )(q, *([kv] * (2 * n_parts)), bias)


def _attn_step_kernel(q_ref, ck_ref, cv_ref, kn_ref, vn_ref, bias_ref, o_in_ref, o_ref, *, n_heads, head_dim):
    del o_in_ref
    n_cache = ck_ref.shape[0] // n_heads
    for h in range(n_heads):
        hs = slice(h * head_dim, (h + 1) * head_dim)
        qh = q_ref[:, hs]
        kc = ck_ref[pl.ds(h, n_cache, stride=n_heads), :].astype(BF16)
        vc = cv_ref[pl.ds(h, n_cache, stride=n_heads), :].astype(BF16)
        s_parts = [_qk(qh, kc) + bias_ref[h, :, :n_cache], _qk(qh, kn_ref[:, hs]) + bias_ref[h, :, n_cache:]]
        o_ref[:, hs] = _softmax_pv(s_parts, [vc, vn_ref[:, hs]]).astype(o_ref.dtype)


def attn_step_call(q, cache_k, cache_v, kv, bias, o_full, row0, n_heads, head_dim):
    B, n_cache = cache_k.shape[:2]
    D = n_heads * head_dim
    S = CHUNK
    blk0 = row0 // S
    ck = cache_k.reshape(B, n_cache * n_heads, head_dim)
    cv = cache_v.reshape(B, n_cache * n_heads, head_dim)
    kern = functools.partial(_attn_step_kernel, n_heads=n_heads, head_dim=head_dim)
    newspec = lambda col: pl.BlockSpec((S, D), lambda b, col=col: (blk0 + b, col))
    return pl.pallas_call(
        kern,
        grid=(B,),
        in_specs=[newspec(0),
                  pl.BlockSpec((None, n_cache * n_heads, head_dim), lambda b: (b, 0, 0)),
                  pl.BlockSpec((None, n_cache * n_heads, head_dim), lambda b: (b, 0, 0)),
                  newspec(0), newspec(1),
                  pl.BlockSpec(bias.shape, lambda b: (0, 0, 0)),
                  pl.BlockSpec(memory_space=pl.ANY)],
        out_specs=newspec(0),
        out_shape=jax.ShapeDtypeStruct(o_full.shape, o_full.dtype),
        input_output_aliases={6: 0},
        compiler_params=_params(("arbitrary",)),
        name="attn_step",
    )(q, ck, cv, kv, kv, bias, o_full)


def band_bias(table):
    H = table.shape[0]
    tq = ATTN_GROUP * CHUNK
    nk = tq + N_BACK_CHUNKS * CHUNK
    n_parts = nk // tq
    P = tq + nk
    d = jnp.concatenate([-jnp.arange(nk), jnp.zeros((1,), jnp.int32), jnp.arange(tq - 1, 0, -1)])
    w = table[:, jnp.clip(N_BACK_CHUNKS * CHUNK + d, -REL_CLIP, REL_CLIP) + REL_CLIP].astype(F32) * LOG2E
    b = jnp.tile(w, (1, tq))[:, :tq * (P - 1)].reshape(H, tq, P - 1)[:, :, :nk]
    i = jnp.arange(tq)[:, None]
    j = jnp.arange(nk)[None, :]
    qc, jc = i // CHUNK, j // CHUNK
    band = (jc >= qc) & (jc <= qc + N_BACK_CHUNKS)
    v = jnp.arange(n_parts)[:, None, None]
    ok = band[None] & (j[None] >= (n_parts - 1 - v) * tq)
    return jnp.where(ok[:, None], b[None], -jnp.inf)


def _roll_cache_kernel(c_ref, new_ref, o_ref, *, n_heads, head_dim, n_new):
    keep = c_ref.shape[0] - n_new * n_heads
    o_ref[0:keep, :] = c_ref[n_new * n_heads:, :]
    new = new_ref[...]
    for h in range(n_heads):
        o_ref[pl.ds(keep + h, n_new, stride=n_heads), :] = new[:, h * head_dim:(h + 1) * head_dim]


def roll_cache_call(cache, new_rows, row0, col):
    B, n_cache, n_heads, head_dim = cache.shape
    D = n_heads * head_dim
    S = CHUNK
    blk0 = row0 // S
    c = cache.reshape(B, n_cache * n_heads, head_dim)
    out = pl.pallas_call(
        functools.partial(_roll_cache_kernel, n_heads=n_heads, head_dim=head_dim, n_new=S),
        grid=(B,),
        in_specs=[pl.BlockSpec((None, n_cache * n_heads, head_dim), lambda b: (b, 0, 0)),
                  pl.BlockSpec((S, D), lambda b: (blk0 + b, col))],
        out_specs=pl.BlockSpec((None, n_cache * n_heads, head_dim), lambda b: (b, 0, 0)),
        out_shape=jax.ShapeDtypeStruct(c.shape, cache.dtype),
        compiler_params=_params(("arbitrary",)),
        name="roll_cache",
    )(c, new_rows)
    return out.reshape(cache.shape)


def kernel(x_prompt, x_sample, state_conv, cache_k, cache_v, g_mix, g_ffn, w_in_a, conv_a, w_out_a, g_kv, w_kv,
           w_q_b, w_o_b, rel_bias_b, w_group, b_group, w_router, b_router, w_gate, w_up, w_down, g_final):
    Bp, Tp, D = x_prompt.shape
    Bs, Ss, _ = x_sample.shape
    n_a = w_in_a.shape[0]
    depth = g_mix.shape[0]
    n_heads, head_dim = cache_k.shape[2], cache_k.shape[3]
    n_cache = cache_k.shape[1]
    assert Bp == 1 and Ss == CHUNK
    n_p = Bp * Tp
    n_s = Bs * Ss
    M = n_p + n_s
    n_tiles_max = (2 * M) // EXPERT_TILE + N_EXPERTS
    n_slots = n_tiles_max * EXPERT_TILE

    def moe(x1, xn2, route, route_t, counts, layer, gains, gain_dtypes, emit_x, split_rows=0):
        dest, tile_expert, n_tiles, pad_start, pad_len = routing_plan(route_t, counts, n_tiles_max)
        xs = dispatch_call(xn2, dest, pad_start, pad_len, n_tiles, n_slots)
        ys = experts_call(xs, tile_expert, n_tiles, w_gate, w_up, w_down, layer)
        return combine_call(x1, route, ys, dest, gains, gain_dtypes, emit_x, split_rows)

    def router_weights(layer):
        pad = LANES - N_GROUPS - N_EXPERTS
        w = jnp.concatenate([w_group[layer], w_router[layer], jnp.zeros((D, pad), F32)], axis=1)
        w_hi = w.astype(BF16)
        w_lo = (w - w_hi.astype(F32)).astype(BF16)
        w = jnp.concatenate([w_hi, w_lo], axis=1)
        b = jnp.concatenate([b_group[layer], b_router[layer], jnp.zeros((pad,), F32)]).reshape(1, LANES)
        return w, b

    x, xn = norm_in_call(x_prompt.reshape(n_p, D), x_sample.reshape(n_s, D), g_mix[0])
    conv_p, conv_s = [], []
    for l in range(n_a):
        st = state_conv[l]
        z = jnp.zeros((Bs, Ss - 1, D), F32)
        inj1 = jnp.concatenate([st[:, 1:2], z], axis=1).reshape(n_s, D)
        inj2 = jnp.concatenate([st, z[:, 1:]], axis=1).reshape(n_s, D)
        g, cp, cs = conv_in_call(xn, w_in_a, conv_a, l, inj1, inj2, n_p, Bs, Ss)
        conv_p.append(cp.reshape(Bp, 2, D))
        conv_s.append(cs)
        wr, br = router_weights(l)
        routed = proj_route_call(g, w_out_a, (l,), x, g_ffn[l], wr, br)
        if l + 1 < n_a:
            x, xn = moe(*routed, l, [g_mix[l + 1]], [BF16], True)
        else:
            x, xn_kv, xn = moe(*routed, l, [g_kv, g_mix[l + 1]], [BF16, BF16], True)

    keep_p = min(N_BACK_CHUNKS * CHUNK, Tp)
    mm_tile = 512
    assert keep_p == mm_tile and n_s == mm_tile
    kv_bf, kv_tail = matmul_call(xn_kv, w_kv, (), [BF16, F32], tail_tiles=(0, 2), tm=mm_tile, name="kv_proj")

    y_prompt = y_sample = None
    for j in range(depth - n_a):
        l = n_a + j
        (q,) = matmul_call(xn, w_q_b, (j,), [BF16], out_scale=head_dim ** -0.5 * LOG2E, name="q_proj")
        bias = band_bias(rel_bias_b[j])
        o = attn_prompt_call(q, kv_bf, bias, n_p, n_heads, head_dim)
        o = attn_step_call(q, cache_k, cache_v, kv_bf, bias[-1, :, :Ss, :n_cache + Ss], o, n_p, n_heads, head_dim)
        wr, br = router_weights(l)
        routed = proj_route_call(o, w_o_b, (j,), x, g_ffn[l], wr, br)
        if l + 1 < depth:
            x, xn = moe(*routed, l, [g_mix[l + 1]], [BF16], True)
        else:
            y_prompt, y_sample = moe(*routed, l, [g_final], [F32], False, split_rows=n_p)

    new_conv_prompt = jnp.stack(conv_p, axis=0)
    new_conv_sample = jnp.stack(conv_s, axis=0)
    new_k_prompt = kv_tail[:keep_p, :D].reshape(Bp, keep_p, n_heads, head_dim)
    new_v_prompt = kv_tail[:keep_p, D:].reshape(Bp, keep_p, n_heads, head_dim)
    new_k_sample = roll_cache_call(cache_k, kv_tail, keep_p, 0)
    new_v_sample = roll_cache_call(cache_v, kv_tail, keep_p, 1)
    return (y_prompt.reshape(Bp, Tp, D), y_sample.reshape(Bs, Ss, D), new_conv_prompt, new_k_prompt, new_v_prompt,
            new_conv_sample, new_k_sample, new_v_sample)
```

```python
import functools
import math

import jax
import jax.numpy as jnp
from jax import lax
from jax.experimental import pallas as pl
from jax.experimental.pallas import tpu as pltpu

F32 = jnp.float32
BF16 = jnp.bfloat16

CHUNK = 64
N_BACK_CHUNKS = 8
REL_CLIP = 128
N_GROUPS = 4
EXPERTS_PER_GROUP = 4
N_EXPERTS = N_GROUPS * EXPERTS_PER_GROUP
EPS = 1e-6

LANES = 128
SUBLANES = 8
EXPERT_TILE = 256
ATTN_GROUP = 4
VMEM_LIMIT = 56 * 1024 * 1024
LOG2E = math.log2(math.e)


def _params(sem, **kw):
    return pltpu.CompilerParams(dimension_semantics=sem, vmem_limit_bytes=VMEM_LIMIT, **kw)


def _rms(x):
    return x * lax.rsqrt(jnp.mean(x * x, axis=-1, keepdims=True) + EPS)


def _norm_in_kernel(xa_ref, xb_ref, g_ref, x_ref, xn_ref, *, a_tiles):
    def emit(src):
        x = src[...]
        x_ref[...] = x
        xn_ref[...] = (_rms(x) * g_ref[...]).astype(xn_ref.dtype)

    @pl.when(pl.program_id(0) < a_tiles)
    def _():
        emit(xa_ref)

    @pl.when(pl.program_id(0) >= a_tiles)
    def _():
        emit(xb_ref)


def norm_in_call(xa, xb, g, tm=512):
    (na, D), nb = xa.shape, xb.shape[0]
    a_tiles = na // tm
    assert na % tm == 0 and nb % tm == 0
    M = na + nb
    return pl.pallas_call(
        functools.partial(_norm_in_kernel, a_tiles=a_tiles),
        grid=(M // tm,),
        in_specs=[pl.BlockSpec((tm, D), lambda i: (jnp.minimum(i, a_tiles - 1), 0)),
                  pl.BlockSpec((tm, D), lambda i: (jnp.maximum(i - a_tiles, 0), 0)),
                  pl.BlockSpec((1, D), lambda i: (0, 0))],
        out_specs=[pl.BlockSpec((tm, D), lambda i: (i, 0)), pl.BlockSpec((tm, D), lambda i: (i, 0))],
        out_shape=[jax.ShapeDtypeStruct((M, D), F32), jax.ShapeDtypeStruct((M, D), BF16)],
        compiler_params=_params(("arbitrary",)),
        name="norm_in",
    )(xa, xb, g.reshape(1, D))


def _conv_in_kernel(xn_ref, wb_ref, wc_ref, wh_ref, cw_ref, inj1_ref, inj2_ref,
                    g_ref, cp_ref, cs_ref, wbs, wcs, whs, carry, u_s, *, n_prompt_tiles, n_seq, seq_len):
    i = pl.program_id(1)
    tm = u_s.shape[0]

    @pl.when(i == 0)
    def _():
        wbs[...] = wb_ref[...].astype(BF16)
        wcs[...] = wc_ref[...].astype(BF16)
        whs[...] = wh_ref[...].astype(BF16)
        carry[...] = jnp.zeros_like(carry)

    xn = xn_ref[...]
    b = jnp.dot(xn, wbs[...], preferred_element_type=F32)
    c = jnp.dot(xn, wcs[...], preferred_element_type=F32)
    h = jnp.dot(xn, whs[...], preferred_element_type=F32)
    u = c * h
    u_s[...] = u
    is_sample = i >= n_prompt_tiles
    row = lax.broadcasted_iota(jnp.int32, u.shape, 0)
    rowm = row & jnp.where(is_sample, seq_len - 1, 0x7FFFFFFF)
    c0 = carry[0:1, :]
    c1 = carry[1:2, :]
    e1 = jnp.where(is_sample, inj1_ref[...], c1)
    e2 = jnp.where(is_sample, inj2_ref[...], jnp.where(row == 0, c0, c1))
    up1 = jnp.where(rowm == 0, e1, pltpu.roll(u, 1, 0))
    up2 = jnp.where(rowm < 2, e2, pltpu.roll(u, 2, 0))
    cw = cw_ref[...]
    conv = cw[0:1, :] * up2 + cw[1:2, :] * up1 + cw[2:3, :] * u
    g_ref[...] = (b * conv).astype(g_ref.dtype)
    carry[...] = u_s[tm - 2:tm, :]

    @pl.when(i == n_prompt_tiles - 1)
    def _():
        cp_ref[...] = u_s[tm - 2:tm, :]

    @pl.when(i == n_prompt_tiles)
    def _():
        for s in range(n_seq):
            cs_ref[s] = u_s[(s + 1) * seq_len - 2:(s + 1) * seq_len, :]


def conv_in_call(xn, w_in_all, conv_all, layer, inj1, inj2, n_prompt_rows, n_seq, seq_len, tn=512):
    M, D = xn.shape
    tm = n_seq * seq_len
    assert M == n_prompt_rows + tm and n_prompt_rows % tm == 0 and D % tn == 0
    nj, ni = D // tn, M // tm
    kern = functools.partial(_conv_in_kernel, n_prompt_tiles=n_prompt_rows // tm, n_seq=n_seq, seq_len=seq_len)
    wspec = lambda k: pl.BlockSpec((None, D, tn), lambda j, i, k=k: (layer, 0, j + k * nj))
    return pl.pallas_call(
        kern,
        grid=(nj, ni),
        in_specs=[
            pl.BlockSpec((tm, D), lambda j, i: (i, 0)),
            wspec(0), wspec(1), wspec(2),
            pl.BlockSpec((None, 3, tn), lambda j, i: (layer, 0, j)),
            pl.BlockSpec((tm, tn), lambda j, i: (0, j)),
            pl.BlockSpec((tm, tn), lambda j, i: (0, j)),
        ],
        out_specs=[
            pl.BlockSpec((tm, tn), lambda j, i: (i, j)),
            pl.BlockSpec((2, tn), lambda j, i: (0, j)),
            pl.BlockSpec((n_seq, 2, tn), lambda j, i: (0, 0, j)),
        ],
        out_shape=[
            jax.ShapeDtypeStruct((M, D), BF16),
            jax.ShapeDtypeStruct((2, D), F32),
            jax.ShapeDtypeStruct((n_seq, 2, D), F32),
        ],
        scratch_shapes=[pltpu.VMEM((D, tn), BF16)] * 3 + [pltpu.VMEM((2, tn), F32), pltpu.VMEM((tm, tn), F32)],
        compiler_params=_params(("arbitrary", "arbitrary")),
        name="conv_in",
    )(xn, w_in_all, w_in_all, w_in_all, conv_all, inj1, inj2)


def _matmul_kernel(x_ref, w_ref, *rest, out_scale):
    out_refs, ws = rest[:-1], rest[-1]

    @pl.when(pl.program_id(1) == 0)
    def _():
        ws[...] = w_ref[...].astype(BF16)

    y = jnp.dot(x_ref[...], ws[...], preferred_element_type=F32)
    if out_scale is not None:
        y = y * out_scale
    for o in out_refs:
        o[...] = y.astype(o.dtype)


def matmul_call(x, w, w_index, out_dtypes, tail_tiles=(), out_scale=None, tm=512, tn=512, name="matmul"):
    M, K = x.shape
    N = w.shape[-1]
    ni = M // tm
    lead = tuple(w_index)
    wblock = (None,) * len(lead) + (K, tn)
    tails = tuple(tail_tiles) + (0,) * (len(out_dtypes) - len(tail_tiles))
    out_specs, out_shape = [], []
    for dt, n in zip(out_dtypes, tails):
        if n:
            out_specs.append(pl.BlockSpec((tm, tn), lambda j, i, n=n: (jnp.maximum(i - (ni - n), 0), j)))
            out_shape.append(jax.ShapeDtypeStruct((n * tm, N), dt))
        else:
            out_specs.append(pl.BlockSpec((tm, tn), lambda j, i: (i, j)))
            out_shape.append(jax.ShapeDtypeStruct((M, N), dt))
    return pl.pallas_call(
        functools.partial(_matmul_kernel, out_scale=out_scale),
        grid=(N // tn, ni),
        in_specs=[pl.BlockSpec((tm, K), lambda j, i: (i, 0)),
                  pl.BlockSpec(wblock, lambda j, i: lead + (0, j))],
        out_specs=out_specs,
        out_shape=out_shape,
        scratch_shapes=[pltpu.VMEM((K, tn), BF16)],
        compiler_params=_params(("arbitrary", "arbitrary")),
        name=name,
    )(x, w)


def _route(logits, carry):
    tm = logits.shape[0]
    lane = lax.broadcasted_iota(jnp.int32, logits.shape, 1).astype(F32)
    neg = -jnp.inf
    big = float(LANES)
    gl = jnp.where(lane < N_GROUPS, logits, neg)
    gmax = jnp.max(gl, axis=-1, keepdims=True)
    gidx = jnp.min(jnp.where(gl == gmax, lane, big), axis=-1, keepdims=True)
    gp = 1.0 / jnp.sum(jnp.exp(gl - gmax), axis=-1, keepdims=True)
    lo = N_GROUPS + gidx * EXPERTS_PER_GROUP
    el = jnp.where(lane >= lo, jnp.where(lane < lo + EXPERTS_PER_GROUP, logits, neg), neg)
    t1 = jnp.max(el, axis=-1, keepdims=True)
    i1 = jnp.min(jnp.where(el == t1, lane, big), axis=-1, keepdims=True)
    el2 = jnp.where(lane == i1, neg, el)
    t2 = jnp.max(el2, axis=-1, keepdims=True)
    i2 = jnp.min(jnp.where(el2 == t2, lane, big), axis=-1, keepdims=True)
    ex = jnp.exp(t2 - t1)
    den = 1.0 + ex
    w0 = (1.0 / den) * gp
    w1 = (ex / den) * gp
    hit1 = lane == i1
    hit2 = lane == i2
    cnt = jnp.where(hit1, 1.0, jnp.where(hit2, 1.0, 0.0))
    r_i = lax.broadcasted_iota(jnp.int32, (tm, tm), 0)
    c_i = lax.broadcasted_iota(jnp.int32, (tm, tm), 1)
    tri = jnp.where(c_i < r_i, 1.0, 0.0).astype(BF16)
    before = jnp.dot(tri, cnt.astype(BF16), preferred_element_type=F32) + carry[...]
    rank0 = jnp.sum(jnp.where(hit1, before, 0.0), axis=-1, keepdims=True)
    rank1 = jnp.sum(jnp.where(hit2, before, 0.0), axis=-1, keepdims=True)
    carry[...] = carry[...] + jnp.sum(cnt, axis=0, keepdims=True)
    vals = (i1 - N_GROUPS, i2 - N_GROUPS, rank0, rank1, w0, w1)
    route = jnp.zeros_like(logits)
    for k, v in enumerate(vals):
        route = jnp.where(lane == float(k), v, route)
    return route


def _proj_route_kernel(a_ref, w_ref, x_ref, gf_ref, wr_ref, br_ref,
                       x1_ref, xn2_ref, route_ref, routet_ref, cnt_ref, ws, carry):
    K = ws.shape[0]

    @pl.when(pl.program_id(0) == 0)
    def _():
        step = 256

        def body(k, _):
            r = pl.multiple_of(k * step, step)
            ws[pl.ds(r, step), :] = w_ref[pl.ds(r, step), :].astype(BF16)
            return 0

        lax.fori_loop(0, K // step, body, 0)
        carry[...] = jnp.zeros_like(carry)

    y = jnp.dot(a_ref[...], ws[...], preferred_element_type=F32)
    x1 = x_ref[...] + y
    x1_ref[...] = x1
    xn = _rms(x1) * gf_ref[...]
    xn2_ref[...] = xn
    xh = xn.astype(BF16)
    xl = (xn - xh.astype(F32)).astype(BF16)
    p1 = jnp.dot(xh, wr_ref[...], preferred_element_type=F32)
    p2 = jnp.dot(xl, wr_ref[:, :LANES], preferred_element_type=F32)
    logits = p1[:, :LANES] + (p1[:, LANES:] + p2) + br_ref[...]
    route = _route(logits, carry)
    route_ref[...] = route
    routet_ref[...] = route.T[:SUBLANES, :]
    cnt_ref[...] = carry[...]


def proj_route_call(a, w_all, w_index, x, g_ffn, w_route, b_route, tm=256):
    M, K = a.shape
    D = x.shape[1]
    lead = tuple(w_index)
    wblock = (None,) * len(lead) + (K, D)
    const = lambda i: (0, 0)
    rowblk = lambda i: (i, 0)
    return pl.pallas_call(
        _proj_route_kernel,
        grid=(M // tm,),
        in_specs=[
            pl.BlockSpec((tm, K), rowblk),
            pl.BlockSpec(wblock, lambda i: lead + (0, 0), pipeline_mode=pl.Buffered(1)),
            pl.BlockSpec((tm, D), rowblk),
            pl.BlockSpec((1, D), const),
            pl.BlockSpec((D, 2 * LANES), const),
            pl.BlockSpec((1, LANES), const),
        ],
        out_specs=[
            pl.BlockSpec((tm, D), rowblk),
            pl.BlockSpec((tm, D), rowblk),
            pl.BlockSpec((tm, LANES), rowblk),
            pl.BlockSpec((SUBLANES, tm), lambda i: (0, i)),
            pl.BlockSpec((1, LANES), const),
        ],
        out_shape=[
            jax.ShapeDtypeStruct((M, D), F32),
            jax.ShapeDtypeStruct((M, D), F32),
            jax.ShapeDtypeStruct((M, LANES), F32),
            jax.ShapeDtypeStruct((SUBLANES, M), F32),
            jax.ShapeDtypeStruct((1, LANES), F32),
        ],
        scratch_shapes=[pltpu.VMEM((K, D), BF16), pltpu.VMEM((1, LANES), F32)],
        compiler_params=_params(("arbitrary",)),
        name="proj_route",
    )(a, w_all, x, g_ffn.reshape(1, D), w_route, b_route)


def routing_plan(route_t, counts, n_tiles_max):
    te = EXPERT_TILE
    eid = route_t[0:2].astype(jnp.int32)
    rank = route_t[2:4].astype(jnp.int32)
    cnt = counts[0, N_GROUPS:N_GROUPS + N_EXPERTS].astype(jnp.int32)
    tiles = (cnt + te - 1) // te
    tile_end = jnp.cumsum(tiles)
    offs = (tile_end - tiles) * te
    dest = rank
    for e in range(N_EXPERTS):
        dest = dest + jnp.where(eid == e, offs[e], 0)
    n_tiles = tile_end[-1]
    tid = jnp.arange(n_tiles_max, dtype=jnp.int32)
    tile_expert = jnp.sum((jnp.minimum(tid, n_tiles - 1)[:, None] >= tile_end[None, :]).astype(jnp.int32), axis=1)
    pad_start = offs + cnt
    pad_len = tiles * te - cnt
    first = jnp.concatenate([jnp.ones((1,), jnp.int32), (tile_expert[1:] != tile_expert[:-1]).astype(jnp.int32)])
    half = (jnp.cumsum(first) - 1) % 2
    nxt_tile = tile_end[tile_expert]
    nxt = jnp.where(nxt_tile < n_tiles, tile_expert[jnp.minimum(nxt_tile, n_tiles_max - 1)], -1)
    return dest.reshape(-1), tile_expert, n_tiles.reshape(1), pad_start, pad_len, (first, half, nxt)


def _slab_row(ref, r):
    return ref.at[r >> 3, pl.ds(r & (SUBLANES - 1), 1), :]


def _dispatch_kernel(dest_ref, pstart_ref, plen_ref, nt_ref, xn_ref, xs_hbm, zbuf, sem, zsem, tsem, *, tm, n_rows):
    i = pl.program_id(0)
    base = i * tm
    tile_slabs = zbuf.shape[0]
    n_tiles_max = xs_hbm.shape[0] // tile_slabs

    def row_copy(k, u, slot):
        return pltpu.make_async_copy(xn_ref.at[k, pl.ds(u, 1), :], _slab_row(xs_hbm, slot), sem)

    def zero_copy(slot):
        return pltpu.make_async_copy(zbuf.at[0, pl.ds(0, 1), :], _slab_row(xs_hbm, slot), zsem)

    def zero_tile_copy(tile):
        return pltpu.make_async_copy(zbuf, xs_hbm.at[pl.ds(tile * tile_slabs, tile_slabs)], tsem)

    def issue(k, _):
        for u in range(SUBLANES):
            r = k * SUBLANES + u
            row_copy(k, u, dest_ref[base + r]).start(priority=0)
            row_copy(k, u, dest_ref[n_rows + base + r]).start(priority=1)
        return 0

    lax.fori_loop(0, tm // SUBLANES, issue, 0)

    @pl.when(i == 0)
    def _():
        zbuf[...] = jnp.zeros_like(zbuf)
        for e in range(N_EXPERTS):
            def zissue(r, _, e=e):
                zero_copy(pstart_ref[e] + r).start()
                return 0

            lax.fori_loop(0, plen_ref[e], zissue, 0)

        def tissue(t, _):
            zero_tile_copy(t).start()
            return 0

        lax.fori_loop(nt_ref[0], n_tiles_max, tissue, 0)
        for e in range(N_EXPERTS):
            def zwait(r, _):
                zero_copy(0).wait()
                return 0

            lax.fori_loop(0, plen_ref[e], zwait, 0)

        def twait(t, _):
            zero_tile_copy(0).wait()
            return 0

        lax.fori_loop(nt_ref[0], n_tiles_max, twait, 0)

    def wait(k, _):
        for u in range(2 * SUBLANES):
            row_copy(0, 0, 0).wait()
        return 0

    lax.fori_loop(0, tm // SUBLANES, wait, 0)


def dispatch_call(xn, dest, pad_start, pad_len, n_tiles, n_slots, tm=512):
    M, D = xn.shape
    kern = functools.partial(_dispatch_kernel, tm=tm, n_rows=M)
    xs = pl.pallas_call(
        kern,
        grid_spec=pltpu.PrefetchScalarGridSpec(
            num_scalar_prefetch=4,
            grid=(M // tm,),
            in_specs=[pl.BlockSpec((tm // SUBLANES, SUBLANES, D), lambda i, *_: (i, 0, 0))],
            out_specs=pl.BlockSpec(memory_space=pl.ANY),
            scratch_shapes=[pltpu.VMEM((EXPERT_TILE // SUBLANES, SUBLANES, D), xn.dtype)]
            + [pltpu.SemaphoreType.DMA(())] * 3,
        ),
        out_shape=jax.ShapeDtypeStruct((n_slots // SUBLANES, SUBLANES, D), xn.dtype),
        compiler_params=_params(("arbitrary",)),
        name="dispatch",
    )(dest, pad_start, pad_len, n_tiles, xn.reshape(M // SUBLANES, SUBLANES, D))
    return xs.reshape(n_slots, D)


def _experts_kernel(te_ref, nt_ref, first_ref, half_ref, nxt_ref, xs_ref, wg_hbm, wu_hbm, wd_hbm, ys_ref,
                    wgs, wus, wds, sg, su, sd, sems, *, layer):
    t = pl.program_id(0)
    valid = t < nt_ref[0]

    def fetch(e, half):
        return (pltpu.make_async_copy(wg_hbm.at[layer, e], sg.at[half], sems.at[half]),
                pltpu.make_async_copy(wu_hbm.at[layer, e], su.at[half], sems.at[half]),
                pltpu.make_async_copy(wd_hbm.at[layer, e], sd.at[half], sems.at[half]))

    @pl.when(t == 0)
    def _():
        for c in fetch(te_ref[0], 0):
            c.start()

    @pl.when(jnp.logical_and(valid, first_ref[t] == 1))
    def _():
        half = half_ref[t]
        for c in fetch(0, half):
            c.wait()

        @pl.when(nxt_ref[t] >= 0)
        def _():
            for c in fetch(nxt_ref[t], 1 - half):
                c.start()

        wgs[...] = sg[half].astype(BF16)
        wus[...] = su[half].astype(BF16)
        wds[...] = sd[half].astype(BF16)

    @pl.when(valid)
    def _():
        x = xs_ref[...].astype(BF16)
        hg = jnp.dot(x, wgs[...], preferred_element_type=F32)
        hu = jnp.dot(x, wus[...], preferred_element_type=F32)
        h = (hg * (1.0 / (1.0 + jnp.exp(-hg)))) * hu
        ys_ref[...] = jnp.dot(h.astype(BF16), wds[...], preferred_element_type=F32)

    @pl.when(jnp.logical_not(valid))
    def _():
        ys_ref[...] = jnp.zeros_like(ys_ref)


def experts_call(xs, tile_expert, n_tiles, weight_plan, w_gate, w_up, w_down, layer):
    S, D = xs.shape
    Fd = w_gate.shape[-1]
    te = EXPERT_TILE
    anyspec = pl.BlockSpec(memory_space=pl.ANY)
    return pl.pallas_call(
        functools.partial(_experts_kernel, layer=layer),
        grid_spec=pltpu.PrefetchScalarGridSpec(
            num_scalar_prefetch=5,
            grid=(S // te,),
            in_specs=[pl.BlockSpec((te, D), lambda t, te_ref, nt_ref, *_: (jnp.minimum(t, nt_ref[0] - 1), 0)),
                      anyspec, anyspec, anyspec],
            out_specs=pl.BlockSpec((te, D), lambda t, *_: (t, 0)),
            scratch_shapes=[pltpu.VMEM((D, Fd), BF16), pltpu.VMEM((D, Fd), BF16), pltpu.VMEM((Fd, D), BF16),
                            pltpu.VMEM((2, D, Fd), F32), pltpu.VMEM((2, D, Fd), F32), pltpu.VMEM((2, Fd, D), F32),
                            pltpu.SemaphoreType.DMA((2,))],
        ),
        out_shape=jax.ShapeDtypeStruct((S, D), F32),
        compiler_params=_params(("arbitrary",)),
        name="experts",
    )(tile_expert, n_tiles, *weight_plan, xs, w_gate, w_up, w_down)


def _combine_kernel(dest_ref, x1_ref, route_ref, ys_hbm, *rest, tm, n_rows, n_gain, emit_x, split_tiles):
    gain_refs = rest[:n_gain]
    n_out = (1 if emit_x else 0) + (2 if split_tiles else n_gain)
    out_refs = rest[n_gain:n_gain + n_out]
    buf0, buf1, sems = rest[n_gain + n_out:]
    i = pl.program_id(0)
    D = x1_ref.shape[1]
    cur = i % 2

    def row_copy(slot, buf, half, k, u):
        return pltpu.make_async_copy(_slab_row(ys_hbm, slot), buf.at[half, k, pl.ds(u, 1), :], sems.at[half])

    def gather(tile, half):
        base = tile * tm

        def issue(k, _):
            for u in range(SUBLANES):
                r = k * SUBLANES + u
                row_copy(dest_ref[base + r], buf0, half, k, u).start(priority=0)
                row_copy(dest_ref[n_rows + base + r], buf1, half, k, u).start(priority=1)
            return 0

        lax.fori_loop(0, tm // SUBLANES, issue, 0)

    @pl.when(i == 0)
    def _():
        gather(0, 0)

    @pl.when(i + 1 < pl.num_programs(0))
    def _():
        gather(i + 1, 1 - cur)

    def wait(k, _):
        for u in range(SUBLANES):
            row_copy(0, buf0, cur, 0, 0).wait()
            row_copy(0, buf1, cur, 0, 0).wait()
        return 0

    lax.fori_loop(0, tm // SUBLANES, wait, 0)

    route = route_ref[...]
    y0 = buf0[cur].reshape(tm, D)
    y1 = buf1[cur].reshape(tm, D)
    x2 = x1_ref[...] + (route[:, 4:5] * y0 + route[:, 5:6] * y1)
    k = 0
    if emit_x:
        out_refs[0][...] = x2
        k = 1
    if n_gain:
        xh = _rms(x2)
        if split_tiles:
            y = (xh * gain_refs[0][...]).astype(out_refs[k].dtype)

            @pl.when(i < split_tiles)
            def _():
                out_refs[k][...] = y

            @pl.when(i >= split_tiles)
            def _():
                out_refs[k + 1][...] = y
        else:
            for g_ref, o_ref in zip(gain_refs, out_refs[k:]):
                o_ref[...] = (xh * g_ref[...]).astype(o_ref.dtype)


def combine_call(x1, route, ys, dest, gains, gain_dtypes, emit_x, split_rows=0, tm=256):
    M, D = x1.shape
    n_gain = len(gains)
    split_tiles = split_rows // tm
    kern = functools.partial(_combine_kernel, tm=tm, n_rows=M, n_gain=n_gain, emit_x=emit_x, split_tiles=split_tiles)
    rowblk = lambda i, d: (i, 0)
    const = lambda i, d: (0, 0)
    out_specs = [pl.BlockSpec((tm, D), rowblk)] if emit_x else []
    out_shape = [jax.ShapeDtypeStruct((M, D), F32)] if emit_x else []
    if split_tiles:
        assert n_gain == 1 and split_rows % tm == 0
        out_specs += [pl.BlockSpec((tm, D), lambda i, d: (jnp.minimum(i, split_tiles - 1), 0)),
                      pl.BlockSpec((tm, D), lambda i, d: (jnp.maximum(i - split_tiles, 0), 0))]
        out_shape += [jax.ShapeDtypeStruct((split_rows, D), gain_dtypes[0]),
                      jax.ShapeDtypeStruct((M - split_rows, D), gain_dtypes[0])]
    else:
        out_specs += [pl.BlockSpec((tm, D), rowblk) for _ in gain_dtypes]
        out_shape += [jax.ShapeDtypeStruct((M, D), dt) for dt in gain_dtypes]
    slabs = (2, tm // SUBLANES, SUBLANES, D)
    return pl.pallas_call(
        kern,
        grid_spec=pltpu.PrefetchScalarGridSpec(
            num_scalar_prefetch=1,
            grid=(M // tm,),
            in_specs=[pl.BlockSpec((tm, D), rowblk), pl.BlockSpec((tm, LANES), rowblk),
                      pl.BlockSpec(memory_space=pl.ANY)] + [pl.BlockSpec((1, D), const)] * n_gain,
            out_specs=out_specs,
            scratch_shapes=[pltpu.VMEM(slabs, F32), pltpu.VMEM(slabs, F32), pltpu.SemaphoreType.DMA((2,))],
        ),
        out_shape=out_shape,
        compiler_params=_params(("arbitrary",)),
        name="combine",
    )(dest, x1, route, ys.reshape(ys.shape[0] // SUBLANES, SUBLANES, D), *[g.reshape(1, D) for g in gains])


def _softmax_pv(s_parts, v_parts):
    m = s_parts[0].max(axis=-1, keepdims=True)
    for s in s_parts[1:]:
        m = jnp.maximum(m, s.max(axis=-1, keepdims=True))
    l = None
    o = None
    for s, v in zip(s_parts, v_parts):
        e = jnp.exp2(s - m)
        ls = jnp.sum(e, axis=-1, keepdims=True)
        ov = jnp.dot(e.astype(BF16), v, preferred_element_type=F32)
        l = ls if l is None else l + ls
        o = ov if o is None else o + ov
    return o / l


def _qk(qh, kh):
    return lax.dot_general(qh, kh, (((1,), (1,)), ((), ())), preferred_element_type=F32)


def _attn_prompt_kernel(q_ref, *rest, n_parts, n_heads, head_dim, n_blocks):
    k_refs = rest[:n_parts]
    v_refs = rest[n_parts:2 * n_parts]
    bias_ref, o_ref = rest[2 * n_parts:]
    pw = k_refs[0].shape[0]

    @pl.when(pl.program_id(0) < n_blocks)
    def _():
        for h in range(n_heads):
            hs = slice(h * head_dim, (h + 1) * head_dim)
            qh = q_ref[:, hs]
            s_parts = [_qk(qh, k_refs[p][:, hs]) + bias_ref[h, :, p * pw:(p + 1) * pw] for p in range(n_parts)]
            o_ref[:, hs] = _softmax_pv(s_parts, [v_refs[p][:, hs] for p in range(n_parts)]).astype(o_ref.dtype)

    @pl.when(pl.program_id(0) >= n_blocks)
    def _():
        o_ref[...] = jnp.zeros_like(o_ref)


def attn_prompt_call(q, kv, bias, n_rows, n_heads, head_dim):
    D = n_heads * head_dim
    tq = ATTN_GROUP * CHUNK
    n_parts = (N_BACK_CHUNKS * CHUNK) // tq + 1
    kern = functools.partial(_attn_prompt_kernel, n_parts=n_parts, n_heads=n_heads, head_dim=head_dim,
                             n_blocks=n_rows // tq)
    kspec = lambda p, col: pl.BlockSpec(
        (tq, D), lambda b, p=p, col=col: (jnp.maximum(b - (n_parts - 1) + p, 0), col))
    return pl.pallas_call(
        kern,
        grid=(q.shape[0] // tq,),
        in_specs=[pl.BlockSpec((tq, D), lambda b: (b, 0))]
        + [kspec(p, 0) for p in range(n_parts)] + [kspec(p, 1) for p in range(n_parts)]
        + [pl.BlockSpec((None,) + bias.shape[1:], lambda b: (jnp.minimum(b, n_parts - 1), 0, 0, 0),
                        pipeline_mode=pl.Buffered(1))],
        out_specs=pl.BlockSpec((tq, D), lambda b: (b, 0)),
        out_shape=jax.ShapeDtypeStruct((q.shape[0], D), BF16),
        compiler_params=_params(("arbitrary",)),
        name="attn_prompt",
    )(q, *([kv] * (2 * n_parts)), bias)


def _attn_step_kernel(q_ref, ck_ref, cv_ref, kn_ref, vn_ref, bias_ref, o_in_ref, o_ref, *, n_heads, head_dim):
    del o_in_ref
    n_cache = ck_ref.shape[0] // n_heads
    for h in range(n_heads):
        hs = slice(h * head_dim, (h + 1) * head_dim)
        qh = q_ref[:, hs]
        kc = ck_ref[pl.ds(h, n_cache, stride=n_heads), :].astype(BF16)
        vc = cv_ref[pl.ds(h, n_cache, stride=n_heads), :].astype(BF16)
        s_parts = [_qk(qh, kc) + bias_ref[h, :, :n_cache], _qk(qh, kn_ref[:, hs]) + bias_ref[h, :, n_cache:]]
        o_ref[:, hs] = _softmax_pv(s_parts, [vc, vn_ref[:, hs]]).astype(o_ref.dtype)


def attn_step_call(q, cache_k, cache_v, kv, bias, o_full, row0, n_heads, head_dim):
    B, n_cache = cache_k.shape[:2]
    D = n_heads * head_dim
    S = CHUNK
    blk0 = row0 // S
    ck = cache_k.reshape(B, n_cache * n_heads, head_dim)
    cv = cache_v.reshape(B, n_cache * n_heads, head_dim)
    kern = functools.partial(_attn_step_kernel, n_heads=n_heads, head_dim=head_dim)
    newspec = lambda col: pl.BlockSpec((S, D), lambda b, col=col: (blk0 + b, col))
    return pl.pallas_call(
        kern,
        grid=(B,),
        in_specs=[newspec(0),
                  pl.BlockSpec((None, n_cache * n_heads, head_dim), lambda b: (b, 0, 0)),
                  pl.BlockSpec((None, n_cache * n_heads, head_dim), lambda b: (b, 0, 0)),
                  newspec(0), newspec(1),
                  pl.BlockSpec(bias.shape, lambda b: (0, 0, 0)),
                  pl.BlockSpec(memory_space=pl.ANY)],
        out_specs=newspec(0),
        out_shape=jax.ShapeDtypeStruct(o_full.shape, o_full.dtype),
        input_output_aliases={6: 0},
        compiler_params=_params(("arbitrary",)),
        name="attn_step",
    )(q, ck, cv, kv, kv, bias, o_full)


def band_bias(table):
    H = table.shape[0]
    tq = ATTN_GROUP * CHUNK
    nk = tq + N_BACK_CHUNKS * CHUNK
    n_parts = nk // tq
    P = tq + nk
    d = jnp.concatenate([-jnp.arange(nk), jnp.zeros((1,), jnp.int32), jnp.arange(tq - 1, 0, -1)])
    w = table[:, jnp.clip(N_BACK_CHUNKS * CHUNK + d, -REL_CLIP, REL_CLIP) + REL_CLIP].astype(F32) * LOG2E
    b = jnp.tile(w, (1, tq))[:, :tq * (P - 1)].reshape(H, tq, P - 1)[:, :, :nk]
    i = jnp.arange(tq)[:, None]
    j = jnp.arange(nk)[None, :]
    qc, jc = i // CHUNK, j // CHUNK
    band = (jc >= qc) & (jc <= qc + N_BACK_CHUNKS)
    v = jnp.arange(n_parts)[:, None, None]
    ok = band[None] & (j[None] >= (n_parts - 1 - v) * tq)
    return jnp.where(ok[:, None], b[None], -jnp.inf)


def _roll_cache_kernel(c_ref, new_ref, o_ref, *, n_heads, head_dim, n_new):
    keep = c_ref.shape[0] - n_new * n_heads
    o_ref[0:keep, :] = c_ref[n_new * n_heads:, :]
    new = new_ref[...]
    for h in range(n_heads):
        o_ref[pl.ds(keep + h, n_new, stride=n_heads), :] = new[:, h * head_dim:(h + 1) * head_dim]


def roll_cache_call(cache, new_rows, row0, col):
    B, n_cache, n_heads, head_dim = cache.shape
    D = n_heads * head_dim
    S = CHUNK
    blk0 = row0 // S
    c = cache.reshape(B, n_cache * n_heads, head_dim)
    out = pl.pallas_call(
        functools.partial(_roll_cache_kernel, n_heads=n_heads, head_dim=head_dim, n_new=S),
        grid=(B,),
        in_specs=[pl.BlockSpec((None, n_cache * n_heads, head_dim), lambda b: (b, 0, 0)),
                  pl.BlockSpec((S, D), lambda b: (blk0 + b, col))],
        out_specs=pl.BlockSpec((None, n_cache * n_heads, head_dim), lambda b: (b, 0, 0)),
        out_shape=jax.ShapeDtypeStruct(c.shape, cache.dtype),
        compiler_params=_params(("arbitrary",)),
        name="roll_cache",
    )(c, new_rows)
    return out.reshape(cache.shape)


def kernel(x_prompt, x_sample, state_conv, cache_k, cache_v, g_mix, g_ffn, w_in_a, conv_a, w_out_a, g_kv, w_kv,
           w_q_b, w_o_b, rel_bias_b, w_group, b_group, w_router, b_router, w_gate, w_up, w_down, g_final):
    Bp, Tp, D = x_prompt.shape
    Bs, Ss, _ = x_sample.shape
    n_a = w_in_a.shape[0]
    depth = g_mix.shape[0]
    n_heads, head_dim = cache_k.shape[2], cache_k.shape[3]
    n_cache = cache_k.shape[1]
    assert Bp == 1 and Ss == CHUNK
    n_p = Bp * Tp
    n_s = Bs * Ss
    M = n_p + n_s
    n_tiles_max = (2 * M) // EXPERT_TILE + N_EXPERTS
    n_slots = n_tiles_max * EXPERT_TILE

    def moe(x1, xn2, route, route_t, counts, layer, gains, gain_dtypes, emit_x, split_rows=0):
        dest, tile_expert, n_tiles, pad_start, pad_len, weight_plan = routing_plan(route_t, counts, n_tiles_max)
        xs = dispatch_call(xn2, dest, pad_start, pad_len, n_tiles, n_slots)
        ys = experts_call(xs, tile_expert, n_tiles, weight_plan, w_gate, w_up, w_down, layer)
        return combine_call(x1, route, ys, dest, gains, gain_dtypes, emit_x, split_rows)

    def router_weights(layer):
        pad = LANES - N_GROUPS - N_EXPERTS
        w = jnp.concatenate([w_group[layer], w_router[layer], jnp.zeros((D, pad), F32)], axis=1)
        w_hi = w.astype(BF16)
        w_lo = (w - w_hi.astype(F32)).astype(BF16)
        w = jnp.concatenate([w_hi, w_lo], axis=1)
        b = jnp.concatenate([b_group[layer], b_router[layer], jnp.zeros((pad,), F32)]).reshape(1, LANES)
        return w, b

    x, xn = norm_in_call(x_prompt.reshape(n_p, D), x_sample.reshape(n_s, D), g_mix[0])
    conv_p, conv_s = [], []
    for l in range(n_a):
        st = state_conv[l]
        z = jnp.zeros((Bs, Ss - 1, D), F32)
        inj1 = jnp.concatenate([st[:, 1:2], z], axis=1).reshape(n_s, D)
        inj2 = jnp.concatenate([st, z[:, 1:]], axis=1).reshape(n_s, D)
        g, cp, cs = conv_in_call(xn, w_in_a, conv_a, l, inj1, inj2, n_p, Bs, Ss)
        conv_p.append(cp.reshape(Bp, 2, D))
        conv_s.append(cs)
        wr, br = router_weights(l)
        routed = proj_route_call(g, w_out_a, (l,), x, g_ffn[l], wr, br)
        if l + 1 < n_a:
            x, xn = moe(*routed, l, [g_mix[l + 1]], [BF16], True)
        else:
            x, xn_kv, xn = moe(*routed, l, [g_kv, g_mix[l + 1]], [BF16, BF16], True)

    keep_p = min(N_BACK_CHUNKS * CHUNK, Tp)
    mm_tile = 512
    assert keep_p == mm_tile and n_s == mm_tile
    kv_bf, kv_tail = matmul_call(xn_kv, w_kv, (), [BF16, F32], tail_tiles=(0, 2), tm=mm_tile, name="kv_proj")

    y_prompt = y_sample = None
    for j in range(depth - n_a):
        l = n_a + j
        (q,) = matmul_call(xn, w_q_b, (j,), [BF16], out_scale=head_dim ** -0.5 * LOG2E, name="q_proj")
        bias = band_bias(rel_bias_b[j])
        o = attn_prompt_call(q, kv_bf, bias, n_p, n_heads, head_dim)
        o = attn_step_call(q, cache_k, cache_v, kv_bf, bias[-1, :, :Ss, :n_cache + Ss], o, n_p, n_heads, head_dim)
        wr, br = router_weights(l)
        routed = proj_route_call(o, w_o_b, (j,), x, g_ffn[l], wr, br)
        if l + 1 < depth:
            x, xn = moe(*routed, l, [g_mix[l + 1]], [BF16], True)
        else:
            y_prompt, y_sample = moe(*routed, l, [g_final], [F32], False, split_rows=n_p)

    new_conv_prompt = jnp.stack(conv_p, axis=0)
    new_conv_sample = jnp.stack(conv_s, axis=0)
    new_k_prompt = kv_tail[:keep_p, :D].reshape(Bp, keep_p, n_heads, head_dim)
    new_v_prompt = kv_tail[:keep_p, D:].reshape(Bp, keep_p, n_heads, head_dim)
    new_k_sample = roll_cache_call(cache_k, kv_tail, keep_p, 0)
    new_v_sample = roll_cache_call(cache_v, kv_tail, keep_p, 1)
    return (y_prompt.reshape(Bp, Tp, D), y_sample.reshape(Bs, Ss, D), new_conv_prompt, new_k_prompt, new_v_prompt,
            new_conv_sample, new_k_sample, new_v_sample)
```

```python
import functools
import math

import jax
import jax.numpy as jnp
from jax import lax
from jax.experimental import pallas as pl
from jax.experimental.pallas import tpu as pltpu

F32 = jnp.float32
BF16 = jnp.bfloat16

CHUNK = 64
N_BACK_CHUNKS = 8
REL_CLIP = 128
N_GROUPS = 4
EXPERTS_PER_GROUP = 4
N_EXPERTS = N_GROUPS * EXPERTS_PER_GROUP
EPS = 1e-6

LANES = 128
SUBLANES = 8
EXPERT_TILE = 256
ATTN_GROUP = 4
VMEM_LIMIT = 56 * 1024 * 1024
LOG2E = math.log2(math.e)


def _params(sem, **kw):
    return pltpu.CompilerParams(dimension_semantics=sem, vmem_limit_bytes=VMEM_LIMIT, **kw)


def _rms(x):
    return x * lax.rsqrt(jnp.mean(x * x, axis=-1, keepdims=True) + EPS)


def _norm_in_kernel(xa_ref, xb_ref, g_ref, x_ref, xn_ref, *, a_tiles):
    def emit(src):
        x = src[...]
        x_ref[...] = x
        xn_ref[...] = (_rms(x) * g_ref[...]).astype(xn_ref.dtype)

    @pl.when(pl.program_id(0) < a_tiles)
    def _():
        emit(xa_ref)

    @pl.when(pl.program_id(0) >= a_tiles)
    def _():
        emit(xb_ref)


def norm_in_call(xa, xb, g, tm=512):
    (na, D), nb = xa.shape, xb.shape[0]
    a_tiles = na // tm
    assert na % tm == 0 and nb % tm == 0
    M = na + nb
    return pl.pallas_call(
        functools.partial(_norm_in_kernel, a_tiles=a_tiles),
        grid=(M // tm,),
        in_specs=[pl.BlockSpec((tm, D), lambda i: (jnp.minimum(i, a_tiles - 1), 0)),
                  pl.BlockSpec((tm, D), lambda i: (jnp.maximum(i - a_tiles, 0), 0)),
                  pl.BlockSpec((1, D), lambda i: (0, 0))],
        out_specs=[pl.BlockSpec((tm, D), lambda i: (i, 0)), pl.BlockSpec((tm, D), lambda i: (i, 0))],
        out_shape=[jax.ShapeDtypeStruct((M, D), F32), jax.ShapeDtypeStruct((M, D), BF16)],
        compiler_params=_params(("arbitrary",)),
        name="norm_in",
    )(xa, xb, g.reshape(1, D))


def _conv_in_kernel(xn_ref, wb_ref, wc_ref, wh_ref, cw_ref, inj1_ref, inj2_ref,
                    g_ref, cp_ref, cs_ref, wbs, wcs, whs, carry, u_s, *, n_prompt_tiles, n_seq, seq_len):
    i = pl.program_id(1)
    tm = u_s.shape[0]

    @pl.when(i == 0)
    def _():
        wbs[...] = wb_ref[...].astype(BF16)
        wcs[...] = wc_ref[...].astype(BF16)
        whs[...] = wh_ref[...].astype(BF16)
        carry[...] = jnp.zeros_like(carry)

    xn = xn_ref[...]
    b = jnp.dot(xn, wbs[...], preferred_element_type=F32)
    c = jnp.dot(xn, wcs[...], preferred_element_type=F32)
    h = jnp.dot(xn, whs[...], preferred_element_type=F32)
    u = c * h
    u_s[...] = u
    is_sample = i >= n_prompt_tiles
    row = lax.broadcasted_iota(jnp.int32, u.shape, 0)
    rowm = row & jnp.where(is_sample, seq_len - 1, 0x7FFFFFFF)
    c0 = carry[0:1, :]
    c1 = carry[1:2, :]
    e1 = jnp.where(is_sample, inj1_ref[...], c1)
    e2 = jnp.where(is_sample, inj2_ref[...], jnp.where(row == 0, c0, c1))
    up1 = jnp.where(rowm == 0, e1, pltpu.roll(u, 1, 0))
    up2 = jnp.where(rowm < 2, e2, pltpu.roll(u, 2, 0))
    cw = cw_ref[...]
    conv = cw[0:1, :] * up2 + cw[1:2, :] * up1 + cw[2:3, :] * u
    g_ref[...] = (b * conv).astype(g_ref.dtype)
    carry[...] = u_s[tm - 2:tm, :]

    @pl.when(i == n_prompt_tiles - 1)
    def _():
        cp_ref[...] = u_s[tm - 2:tm, :]

    @pl.when(i == n_prompt_tiles)
    def _():
        for s in range(n_seq):
            cs_ref[s] = u_s[(s + 1) * seq_len - 2:(s + 1) * seq_len, :]


def conv_in_call(xn, w_in_all, conv_all, layer, inj1, inj2, n_prompt_rows, n_seq, seq_len, tn=512):
    M, D = xn.shape
    tm = n_seq * seq_len
    assert M == n_prompt_rows + tm and n_prompt_rows % tm == 0 and D % tn == 0
    nj, ni = D // tn, M // tm
    kern = functools.partial(_conv_in_kernel, n_prompt_tiles=n_prompt_rows // tm, n_seq=n_seq, seq_len=seq_len)
    wspec = lambda k: pl.BlockSpec((None, D, tn), lambda j, i, k=k: (layer, 0, j + k * nj))
    return pl.pallas_call(
        kern,
        grid=(nj, ni),
        in_specs=[
            pl.BlockSpec((tm, D), lambda j, i: (i, 0)),
            wspec(0), wspec(1), wspec(2),
            pl.BlockSpec((None, 3, tn), lambda j, i: (layer, 0, j)),
            pl.BlockSpec((tm, tn), lambda j, i: (0, j)),
            pl.BlockSpec((tm, tn), lambda j, i: (0, j)),
        ],
        out_specs=[
            pl.BlockSpec((tm, tn), lambda j, i: (i, j)),
            pl.BlockSpec((2, tn), lambda j, i: (0, j)),
            pl.BlockSpec((n_seq, 2, tn), lambda j, i: (0, 0, j)),
        ],
        out_shape=[
            jax.ShapeDtypeStruct((M, D), BF16),
            jax.ShapeDtypeStruct((2, D), F32),
            jax.ShapeDtypeStruct((n_seq, 2, D), F32),
        ],
        scratch_shapes=[pltpu.VMEM((D, tn), BF16)] * 3 + [pltpu.VMEM((2, tn), F32), pltpu.VMEM((tm, tn), F32)],
        compiler_params=_params(("arbitrary", "arbitrary")),
        name="conv_in",
    )(xn, w_in_all, w_in_all, w_in_all, conv_all, inj1, inj2)


def _matmul_kernel(x_ref, w_ref, *rest, out_scale):
    out_refs, ws = rest[:-1], rest[-1]

    @pl.when(pl.program_id(1) == 0)
    def _():
        ws[...] = w_ref[...].astype(BF16)

    y = jnp.dot(x_ref[...], ws[...], preferred_element_type=F32)
    if out_scale is not None:
        y = y * out_scale
    for o in out_refs:
        o[...] = y.astype(o.dtype)


def matmul_call(x, w, w_index, out_dtypes, tail_tiles=(), out_scale=None, tm=512, tn=1024, name="matmul"):
    M, K = x.shape
    N = w.shape[-1]
    ni = M // tm
    lead = tuple(w_index)
    wblock = (None,) * len(lead) + (K, tn)
    tails = tuple(tail_tiles) + (0,) * (len(out_dtypes) - len(tail_tiles))
    out_specs, out_shape = [], []
    for dt, n in zip(out_dtypes, tails):
        if n:
            out_specs.append(pl.BlockSpec((tm, tn), lambda j, i, n=n: (jnp.maximum(i - (ni - n), 0), j)))
            out_shape.append(jax.ShapeDtypeStruct((n * tm, N), dt))
        else:
            out_specs.append(pl.BlockSpec((tm, tn), lambda j, i: (i, j)))
            out_shape.append(jax.ShapeDtypeStruct((M, N), dt))
    return pl.pallas_call(
        functools.partial(_matmul_kernel, out_scale=out_scale),
        grid=(N // tn, ni),
        in_specs=[pl.BlockSpec((tm, K), lambda j, i: (i, 0)),
                  pl.BlockSpec(wblock, lambda j, i: lead + (0, j))],
        out_specs=out_specs,
        out_shape=out_shape,
        scratch_shapes=[pltpu.VMEM((K, tn), BF16)],
        compiler_params=_params(("arbitrary", "arbitrary")),
        name=name,
    )(x, w)


def _route(logits, carry):
    tm = logits.shape[0]
    lane = lax.broadcasted_iota(jnp.int32, logits.shape, 1).astype(F32)
    neg = -jnp.inf
    big = float(LANES)
    gl = jnp.where(lane < N_GROUPS, logits, neg)
    gmax = jnp.max(gl, axis=-1, keepdims=True)
    gidx = jnp.min(jnp.where(gl == gmax, lane, big), axis=-1, keepdims=True)
    gp = 1.0 / jnp.sum(jnp.exp(gl - gmax), axis=-1, keepdims=True)
    lo = N_GROUPS + gidx * EXPERTS_PER_GROUP
    el = jnp.where(lane >= lo, jnp.where(lane < lo + EXPERTS_PER_GROUP, logits, neg), neg)
    t1 = jnp.max(el, axis=-1, keepdims=True)
    i1 = jnp.min(jnp.where(el == t1, lane, big), axis=-1, keepdims=True)
    el2 = jnp.where(lane == i1, neg, el)
    t2 = jnp.max(el2, axis=-1, keepdims=True)
    i2 = jnp.min(jnp.where(el2 == t2, lane, big), axis=-1, keepdims=True)
    ex = jnp.exp(t2 - t1)
    den = 1.0 + ex
    w0 = (1.0 / den) * gp
    w1 = (ex / den) * gp
    hit1 = lane == i1
    hit2 = lane == i2
    cnt = jnp.where(hit1, 1.0, jnp.where(hit2, 1.0, 0.0))
    r_i = lax.broadcasted_iota(jnp.int32, (tm, tm), 0)
    c_i = lax.broadcasted_iota(jnp.int32, (tm, tm), 1)
    tri = jnp.where(c_i < r_i, 1.0, 0.0).astype(BF16)
    before = jnp.dot(tri, cnt.astype(BF16), preferred_element_type=F32) + carry[...]
    rank0 = jnp.sum(jnp.where(hit1, before, 0.0), axis=-1, keepdims=True)
    rank1 = jnp.sum(jnp.where(hit2, before, 0.0), axis=-1, keepdims=True)
    carry[...] = carry[...] + jnp.sum(cnt, axis=0, keepdims=True)
    vals = (i1 - N_GROUPS, i2 - N_GROUPS, rank0, rank1, w0, w1)
    route = jnp.zeros_like(logits)
    for k, v in enumerate(vals):
        route = jnp.where(lane == float(k), v, route)
    return route


def _proj_route_kernel(a_ref, w_ref, x_ref, gf_ref, wr_ref, br_ref,
                       x1_ref, xn2_ref, route_ref, routet_ref, cnt_ref, ws, carry):
    K = ws.shape[0]

    @pl.when(pl.program_id(0) == 0)
    def _():
        step = 256

        def body(k, _):
            r = pl.multiple_of(k * step, step)
            ws[pl.ds(r, step), :] = w_ref[pl.ds(r, step), :].astype(BF16)
            return 0

        lax.fori_loop(0, K // step, body, 0)
        carry[...] = jnp.zeros_like(carry)

    y = jnp.dot(a_ref[...], ws[...], preferred_element_type=F32)
    x1 = x_ref[...] + y
    x1_ref[...] = x1
    xn = _rms(x1) * gf_ref[...]
    xn2_ref[...] = xn
    xh = xn.astype(BF16)
    xl = (xn - xh.astype(F32)).astype(BF16)
    p1 = jnp.dot(xh, wr_ref[...], preferred_element_type=F32)
    p2 = jnp.dot(xl, wr_ref[:, :LANES], preferred_element_type=F32)
    logits = p1[:, :LANES] + (p1[:, LANES:] + p2) + br_ref[...]
    route = _route(logits, carry)
    route_ref[...] = route
    routet_ref[...] = route.T[:SUBLANES, :]
    cnt_ref[...] = carry[...]


def proj_route_call(a, w_all, w_index, x, g_ffn, w_route, b_route, tm=256):
    M, K = a.shape
    D = x.shape[1]
    lead = tuple(w_index)
    wblock = (None,) * len(lead) + (K, D)
    const = lambda i: (0, 0)
    rowblk = lambda i: (i, 0)
    return pl.pallas_call(
        _proj_route_kernel,
        grid=(M // tm,),
        in_specs=[
            pl.BlockSpec((tm, K), rowblk),
            pl.BlockSpec(wblock, lambda i: lead + (0, 0), pipeline_mode=pl.Buffered(1)),
            pl.BlockSpec((tm, D), rowblk),
            pl.BlockSpec((1, D), const),
            pl.BlockSpec((D, 2 * LANES), const),
            pl.BlockSpec((1, LANES), const),
        ],
        out_specs=[
            pl.BlockSpec((tm, D), rowblk),
            pl.BlockSpec((tm, D), rowblk),
            pl.BlockSpec((tm, LANES), rowblk),
            pl.BlockSpec((SUBLANES, tm), lambda i: (0, i)),
            pl.BlockSpec((1, LANES), const),
        ],
        out_shape=[
            jax.ShapeDtypeStruct((M, D), F32),
            jax.ShapeDtypeStruct((M, D), F32),
            jax.ShapeDtypeStruct((M, LANES), F32),
            jax.ShapeDtypeStruct((SUBLANES, M), F32),
            jax.ShapeDtypeStruct((1, LANES), F32),
        ],
        scratch_shapes=[pltpu.VMEM((K, D), BF16), pltpu.VMEM((1, LANES), F32)],
        compiler_params=_params(("arbitrary",)),
        name="proj_route",
    )(a, w_all, x, g_ffn.reshape(1, D), w_route, b_route)


def routing_plan(route_t, counts, n_tiles_max):
    te = EXPERT_TILE
    eid = route_t[0:2].astype(jnp.int32)
    rank = route_t[2:4].astype(jnp.int32)
    cnt = counts[0, N_GROUPS:N_GROUPS + N_EXPERTS].astype(jnp.int32)
    tiles = (cnt + te - 1) // te
    tile_end = jnp.cumsum(tiles)
    offs = (tile_end - tiles) * te
    dest = rank
    for e in range(N_EXPERTS):
        dest = dest + jnp.where(eid == e, offs[e], 0)
    n_tiles = tile_end[-1]
    tid = jnp.arange(n_tiles_max, dtype=jnp.int32)
    tile_expert = jnp.sum((jnp.minimum(tid, n_tiles - 1)[:, None] >= tile_end[None, :]).astype(jnp.int32), axis=1)
    pad_start = offs + cnt
    pad_len = tiles * te - cnt
    first = jnp.concatenate([jnp.ones((1,), jnp.int32), (tile_expert[1:] != tile_expert[:-1]).astype(jnp.int32)])
    half = (jnp.cumsum(first) - 1) % 2
    nxt_tile = tile_end[tile_expert]
    nxt = jnp.where(nxt_tile < n_tiles, tile_expert[jnp.minimum(nxt_tile, n_tiles_max - 1)], -1)
    return dest.reshape(-1), tile_expert, n_tiles.reshape(1), pad_start, pad_len, (first, half, nxt)


def _slab_row(ref, r):
    return ref.at[r >> 3, pl.ds(r & (SUBLANES - 1), 1), :]


def _dispatch_kernel(dest_ref, pstart_ref, plen_ref, nt_ref, xn_ref, xs_hbm, zbuf, sem, zsem, tsem, *, tm, n_rows):
    i = pl.program_id(0)
    base = i * tm
    tile_slabs = zbuf.shape[0]
    n_tiles_max = xs_hbm.shape[0] // tile_slabs

    def row_copy(k, u, slot):
        return pltpu.make_async_copy(xn_ref.at[k, pl.ds(u, 1), :], _slab_row(xs_hbm, slot), sem)

    def zero_copy(slot):
        return pltpu.make_async_copy(zbuf.at[0, pl.ds(0, 1), :], _slab_row(xs_hbm, slot), zsem)

    def zero_tile_copy(tile):
        return pltpu.make_async_copy(zbuf, xs_hbm.at[pl.ds(tile * tile_slabs, tile_slabs)], tsem)

    def issue(k, _):
        for u in range(SUBLANES):
            r = k * SUBLANES + u
            row_copy(k, u, dest_ref[base + r]).start(priority=0)
            row_copy(k, u, dest_ref[n_rows + base + r]).start(priority=1)
        return 0

    lax.fori_loop(0, tm // SUBLANES, issue, 0)

    @pl.when(i == 0)
    def _():
        zbuf[...] = jnp.zeros_like(zbuf)
        for e in range(N_EXPERTS):
            def zissue(r, _, e=e):
                zero_copy(pstart_ref[e] + r).start()
                return 0

            lax.fori_loop(0, plen_ref[e], zissue, 0)

        def tissue(t, _):
            zero_tile_copy(t).start()
            return 0

        lax.fori_loop(nt_ref[0], n_tiles_max, tissue, 0)
        for e in range(N_EXPERTS):
            def zwait(r, _):
                zero_copy(0).wait()
                return 0

            lax.fori_loop(0, plen_ref[e], zwait, 0)

        def twait(t, _):
            zero_tile_copy(0).wait()
            return 0

        lax.fori_loop(nt_ref[0], n_tiles_max, twait, 0)

    def wait(k, _):
        for u in range(2 * SUBLANES):
            row_copy(0, 0, 0).wait()
        return 0

    lax.fori_loop(0, tm // SUBLANES, wait, 0)


def dispatch_call(xn, dest, pad_start, pad_len, n_tiles, n_slots, tm=512):
    M, D = xn.shape
    kern = functools.partial(_dispatch_kernel, tm=tm, n_rows=M)
    xs = pl.pallas_call(
        kern,
        grid_spec=pltpu.PrefetchScalarGridSpec(
            num_scalar_prefetch=4,
            grid=(M // tm,),
            in_specs=[pl.BlockSpec((tm // SUBLANES, SUBLANES, D), lambda i, *_: (i, 0, 0))],
            out_specs=pl.BlockSpec(memory_space=pl.ANY),
            scratch_shapes=[pltpu.VMEM((EXPERT_TILE // SUBLANES, SUBLANES, D), xn.dtype)]
            + [pltpu.SemaphoreType.DMA(())] * 3,
        ),
        out_shape=jax.ShapeDtypeStruct((n_slots // SUBLANES, SUBLANES, D), xn.dtype),
        compiler_params=_params(("arbitrary",)),
        name="dispatch",
    )(dest, pad_start, pad_len, n_tiles, xn.reshape(M // SUBLANES, SUBLANES, D))
    return xs.reshape(n_slots, D)


def _experts_kernel(te_ref, nt_ref, first_ref, half_ref, nxt_ref, xs_ref, wg_hbm, wu_hbm, wd_hbm, ys_ref,
                    wgs, wus, wds, sg, su, sd, sems, *, layer):
    t = pl.program_id(0)
    valid = t < nt_ref[0]

    def fetch(e, half):
        return (pltpu.make_async_copy(wg_hbm.at[layer, e], sg.at[half], sems.at[half]),
                pltpu.make_async_copy(wu_hbm.at[layer, e], su.at[half], sems.at[half]),
                pltpu.make_async_copy(wd_hbm.at[layer, e], sd.at[half], sems.at[half]))

    @pl.when(t == 0)
    def _():
        for c in fetch(te_ref[0], 0):
            c.start()

    @pl.when(jnp.logical_and(valid, first_ref[t] == 1))
    def _():
        half = half_ref[t]
        for c in fetch(0, half):
            c.wait()

        @pl.when(nxt_ref[t] >= 0)
        def _():
            for c in fetch(nxt_ref[t], 1 - half):
                c.start()

        wgs[...] = sg[half].astype(BF16)
        wus[...] = su[half].astype(BF16)
        wds[...] = sd[half].astype(BF16)

    @pl.when(valid)
    def _():
        x = xs_ref[...].astype(BF16)
        hg = jnp.dot(x, wgs[...], preferred_element_type=F32)
        hu = jnp.dot(x, wus[...], preferred_element_type=F32)
        h = (hg * (1.0 / (1.0 + jnp.exp(-hg)))) * hu
        ys_ref[...] = jnp.dot(h.astype(BF16), wds[...], preferred_element_type=F32)

    @pl.when(jnp.logical_not(valid))
    def _():
        ys_ref[...] = jnp.zeros_like(ys_ref)


def experts_call(xs, tile_expert, n_tiles, weight_plan, w_gate, w_up, w_down, layer):
    S, D = xs.shape
    Fd = w_gate.shape[-1]
    te = EXPERT_TILE
    anyspec = pl.BlockSpec(memory_space=pl.ANY)
    return pl.pallas_call(
        functools.partial(_experts_kernel, layer=layer),
        grid_spec=pltpu.PrefetchScalarGridSpec(
            num_scalar_prefetch=5,
            grid=(S // te,),
            in_specs=[pl.BlockSpec((te, D), lambda t, te_ref, nt_ref, *_: (jnp.minimum(t, nt_ref[0] - 1), 0)),
                      anyspec, anyspec, anyspec],
            out_specs=pl.BlockSpec((te, D), lambda t, *_: (t, 0)),
            scratch_shapes=[pltpu.VMEM((D, Fd), BF16), pltpu.VMEM((D, Fd), BF16), pltpu.VMEM((Fd, D), BF16),
                            pltpu.VMEM((2, D, Fd), F32), pltpu.VMEM((2, D, Fd), F32), pltpu.VMEM((2, Fd, D), F32),
                            pltpu.SemaphoreType.DMA((2,))],
        ),
        out_shape=jax.ShapeDtypeStruct((S, D), F32),
        compiler_params=_params(("arbitrary",)),
        name="experts",
    )(tile_expert, n_tiles, *weight_plan, xs, w_gate, w_up, w_down)


def _combine_kernel(dest_ref, x1_ref, route_ref, ys_hbm, *rest, tm, n_rows, n_gain, emit_x, split_tiles):
    gain_refs = rest[:n_gain]
    n_out = (1 if emit_x else 0) + (2 if split_tiles else n_gain)
    out_refs = rest[n_gain:n_gain + n_out]
    buf0, buf1, sems = rest[n_gain + n_out:]
    i = pl.program_id(0)
    D = x1_ref.shape[1]
    cur = i % 2

    def row_copy(slot, buf, half, k, u):
        return pltpu.make_async_copy(_slab_row(ys_hbm, slot), buf.at[half, k, pl.ds(u, 1), :], sems.at[half])

    def gather(tile, half):
        base = tile * tm

        def issue(k, _):
            for u in range(SUBLANES):
                r = k * SUBLANES + u
                row_copy(dest_ref[base + r], buf0, half, k, u).start(priority=0)
                row_copy(dest_ref[n_rows + base + r], buf1, half, k, u).start(priority=1)
            return 0

        lax.fori_loop(0, tm // SUBLANES, issue, 0)

    @pl.when(i == 0)
    def _():
        gather(0, 0)

    @pl.when(i + 1 < pl.num_programs(0))
    def _():
        gather(i + 1, 1 - cur)

    def wait(k, _):
        for u in range(SUBLANES):
            row_copy(0, buf0, cur, 0, 0).wait()
            row_copy(0, buf1, cur, 0, 0).wait()
        return 0

    lax.fori_loop(0, tm // SUBLANES, wait, 0)

    route = route_ref[...]
    y0 = buf0[cur].reshape(tm, D)
    y1 = buf1[cur].reshape(tm, D)
    x2 = x1_ref[...] + (route[:, 4:5] * y0 + route[:, 5:6] * y1)
    k = 0
    if emit_x:
        out_refs[0][...] = x2
        k = 1
    if n_gain:
        xh = _rms(x2)
        if split_tiles:
            y = (xh * gain_refs[0][...]).astype(out_refs[k].dtype)

            @pl.when(i < split_tiles)
            def _():
                out_refs[k][...] = y

            @pl.when(i >= split_tiles)
            def _():
                out_refs[k + 1][...] = y
        else:
            for g_ref, o_ref in zip(gain_refs, out_refs[k:]):
                o_ref[...] = (xh * g_ref[...]).astype(o_ref.dtype)


def combine_call(x1, route, ys, dest, gains, gain_dtypes, emit_x, split_rows=0, tm=256):
    M, D = x1.shape
    n_gain = len(gains)
    split_tiles = split_rows // tm
    kern = functools.partial(_combine_kernel, tm=tm, n_rows=M, n_gain=n_gain, emit_x=emit_x, split_tiles=split_tiles)
    rowblk = lambda i, d: (i, 0)
    const = lambda i, d: (0, 0)
    out_specs = [pl.BlockSpec((tm, D), rowblk)] if emit_x else []
    out_shape = [jax.ShapeDtypeStruct((M, D), F32)] if emit_x else []
    if split_tiles:
        assert n_gain == 1 and split_rows % tm == 0
        out_specs += [pl.BlockSpec((tm, D), lambda i, d: (jnp.minimum(i, split_tiles - 1), 0)),
                      pl.BlockSpec((tm, D), lambda i, d: (jnp.maximum(i - split_tiles, 0), 0))]
        out_shape += [jax.ShapeDtypeStruct((split_rows, D), gain_dtypes[0]),
                      jax.ShapeDtypeStruct((M - split_rows, D), gain_dtypes[0])]
    else:
        out_specs += [pl.BlockSpec((tm, D), rowblk) for _ in gain_dtypes]
        out_shape += [jax.ShapeDtypeStruct((M, D), dt) for dt in gain_dtypes]
    slabs = (2, tm // SUBLANES, SUBLANES, D)
    return pl.pallas_call(
        kern,
        grid_spec=pltpu.PrefetchScalarGridSpec(
            num_scalar_prefetch=1,
            grid=(M // tm,),
            in_specs=[pl.BlockSpec((tm, D), rowblk), pl.BlockSpec((tm, LANES), rowblk),
                      pl.BlockSpec(memory_space=pl.ANY)] + [pl.BlockSpec((1, D), const)] * n_gain,
            out_specs=out_specs,
            scratch_shapes=[pltpu.VMEM(slabs, F32), pltpu.VMEM(slabs, F32), pltpu.SemaphoreType.DMA((2,))],
        ),
        out_shape=out_shape,
        compiler_params=_params(("arbitrary",)),
        name="combine",
    )(dest, x1, route, ys.reshape(ys.shape[0] // SUBLANES, SUBLANES, D), *[g.reshape(1, D) for g in gains])


def _softmax_pv(s_parts, v_parts):
    m = s_parts[0].max(axis=-1, keepdims=True)
    for s in s_parts[1:]:
        m = jnp.maximum(m, s.max(axis=-1, keepdims=True))
    l = None
    o = None
    for s, v in zip(s_parts, v_parts):
        e = jnp.exp2(s - m)
        ls = jnp.sum(e, axis=-1, keepdims=True)
        ov = jnp.dot(e.astype(BF16), v, preferred_element_type=F32)
        l = ls if l is None else l + ls
        o = ov if o is None else o + ov
    return o / l


def _qk(qh, kh):
    return lax.dot_general(qh, kh, (((1,), (1,)), ((), ())), preferred_element_type=F32)


def _attn_prompt_kernel(q_ref, *rest, n_parts, n_heads, head_dim, n_blocks):
    k_refs = rest[:n_parts]
    v_refs = rest[n_parts:2 * n_parts]
    bias_ref, o_ref = rest[2 * n_parts:]
    pw = k_refs[0].shape[0]

    @pl.when(pl.program_id(0) < n_blocks)
    def _():
        for h in range(n_heads):
            hs = slice(h * head_dim, (h + 1) * head_dim)
            qh = q_ref[:, hs]
            s_parts = [_qk(qh, k_refs[p][:, hs]) + bias_ref[h, :, p * pw:(p + 1) * pw] for p in range(n_parts)]
            o_ref[:, hs] = _softmax_pv(s_parts, [v_refs[p][:, hs] for p in range(n_parts)]).astype(o_ref.dtype)

    @pl.when(pl.program_id(0) >= n_blocks)
    def _():
        o_ref[...] = jnp.zeros_like(o_ref)


def attn_prompt_call(q, kv, bias, n_rows, n_heads, head_dim):
    D = n_heads * head_dim
    tq = ATTN_GROUP * CHUNK
    n_parts = (N_BACK_CHUNKS * CHUNK) // tq + 1
    kern = functools.partial(_attn_prompt_kernel, n_parts=n_parts, n_heads=n_heads, head_dim=head_dim,
                             n_blocks=n_rows // tq)
    kspec = lambda p, col: pl.BlockSpec(
        (tq, D), lambda b, p=p, col=col: (jnp.maximum(b - (n_parts - 1) + p, 0), col))
    return pl.pallas_call(
        kern,
        grid=(q.shape[0] // tq,),
        in_specs=[pl.BlockSpec((tq, D), lambda b: (b, 0))]
        + [kspec(p, 0) for p in range(n_parts)] + [kspec(p, 1) for p in range(n_parts)]
        + [pl.BlockSpec((None,) + bias.shape[1:], lambda b: (jnp.minimum(b, n_parts - 1), 0, 0, 0),
                        pipeline_mode=pl.Buffered(1))],
        out_specs=pl.BlockSpec((tq, D), lambda b: (b, 0)),
        out_shape=jax.ShapeDtypeStruct((q.shape[0], D), BF16),
        compiler_params=_params(("arbitrary",)),
        name="attn_prompt",
    )(q, *([kv] * (2 * n_parts)), bias)


def _attn_step_kernel(q_ref, ck_ref, cv_ref, kn_ref, vn_ref, bias_ref, o_in_ref, o_ref, *, n_heads, head_dim):
    del o_in_ref
    n_cache = ck_ref.shape[0] // n_heads
    for h in range(n_heads):
        hs = slice(h * head_dim, (h + 1) * head_dim)
        qh = q_ref[:, hs]
        kc = ck_ref[pl.ds(h, n_cache, stride=n_heads), :].astype(BF16)
        vc = cv_ref[pl.ds(h, n_cache, stride=n_heads), :].astype(BF16)
        s_parts = [_qk(qh, kc) + bias_ref[h, :, :n_cache], _qk(qh, kn_ref[:, hs]) + bias_ref[h, :, n_cache:]]
        o_ref[:, hs] = _softmax_pv(s_parts, [vc, vn_ref[:, hs]]).astype(o_ref.dtype)


def attn_step_call(q, cache_k, cache_v, kv, bias, o_full, row0, n_heads, head_dim):
    B, n_cache = cache_k.shape[:2]
    D = n_heads * head_dim
    S = CHUNK
    blk0 = row0 // S
    ck = cache_k.reshape(B, n_cache * n_heads, head_dim)
    cv = cache_v.reshape(B, n_cache * n_heads, head_dim)
    kern = functools.partial(_attn_step_kernel, n_heads=n_heads, head_dim=head_dim)
    newspec = lambda col: pl.BlockSpec((S, D), lambda b, col=col: (blk0 + b, col))
    return pl.pallas_call(
        kern,
        grid=(B,),
        in_specs=[newspec(0),
                  pl.BlockSpec((None, n_cache * n_heads, head_dim), lambda b: (b, 0, 0)),
                  pl.BlockSpec((None, n_cache * n_heads, head_dim), lambda b: (b, 0, 0)),
                  newspec(0), newspec(1),
                  pl.BlockSpec(bias.shape, lambda b: (0, 0, 0)),
                  pl.BlockSpec(memory_space=pl.ANY)],
        out_specs=newspec(0),
        out_shape=jax.ShapeDtypeStruct(o_full.shape, o_full.dtype),
        input_output_aliases={6: 0},
        compiler_params=_params(("arbitrary",)),
        name="attn_step",
    )(q, ck, cv, kv, kv, bias, o_full)


def band_bias(table):
    H = table.shape[0]
    tq = ATTN_GROUP * CHUNK
    nk = tq + N_BACK_CHUNKS * CHUNK
    n_parts = nk // tq
    P = tq + nk
    d = jnp.concatenate([-jnp.arange(nk), jnp.zeros((1,), jnp.int32), jnp.arange(tq - 1, 0, -1)])
    w = table[:, jnp.clip(N_BACK_CHUNKS * CHUNK + d, -REL_CLIP, REL_CLIP) + REL_CLIP].astype(F32) * LOG2E
    b = jnp.tile(w, (1, tq))[:, :tq * (P - 1)].reshape(H, tq, P - 1)[:, :, :nk]
    i = jnp.arange(tq)[:, None]
    j = jnp.arange(nk)[None, :]
    qc, jc = i // CHUNK, j // CHUNK
    band = (jc >= qc) & (jc <= qc + N_BACK_CHUNKS)
    v = jnp.arange(n_parts)[:, None, None]
    ok = band[None] & (j[None] >= (n_parts - 1 - v) * tq)
    return jnp.where(ok[:, None], b[None], -jnp.inf)


def _roll_cache_kernel(c_ref, new_ref, o_ref, *, n_heads, head_dim, n_new):
    keep = c_ref.shape[0] - n_new * n_heads
    o_ref[0:keep, :] = c_ref[n_new * n_heads:, :]
    new = new_ref[...]
    for h in range(n_heads):
        o_ref[pl.ds(keep + h, n_new, stride=n_heads), :] = new[:, h * head_dim:(h + 1) * head_dim]


def roll_cache_call(cache, new_rows, row0, col):
    B, n_cache, n_heads, head_dim = cache.shape
    D = n_heads * head_dim
    S = CHUNK
    blk0 = row0 // S
    c = cache.reshape(B, n_cache * n_heads, head_dim)
    out = pl.pallas_call(
        functools.partial(_roll_cache_kernel, n_heads=n_heads, head_dim=head_dim, n_new=S),
        grid=(B,),
        in_specs=[pl.BlockSpec((None, n_cache * n_heads, head_dim), lambda b: (b, 0, 0)),
                  pl.BlockSpec((S, D), lambda b: (blk0 + b, col))],
        out_specs=pl.BlockSpec((None, n_cache * n_heads, head_dim), lambda b: (b, 0, 0)),
        out_shape=jax.ShapeDtypeStruct(c.shape, cache.dtype),
        compiler_params=_params(("arbitrary",)),
        name="roll_cache",
    )(c, new_rows)
    return out.reshape(cache.shape)


def kernel(x_prompt, x_sample, state_conv, cache_k, cache_v, g_mix, g_ffn, w_in_a, conv_a, w_out_a, g_kv, w_kv,
           w_q_b, w_o_b, rel_bias_b, w_group, b_group, w_router, b_router, w_gate, w_up, w_down, g_final):
    Bp, Tp, D = x_prompt.shape
    Bs, Ss, _ = x_sample.shape
    n_a = w_in_a.shape[0]
    depth = g_mix.shape[0]
    n_heads, head_dim = cache_k.shape[2], cache_k.shape[3]
    n_cache = cache_k.shape[1]
    assert Bp == 1 and Ss == CHUNK
    n_p = Bp * Tp
    n_s = Bs * Ss
    M = n_p + n_s
    n_tiles_max = (2 * M) // EXPERT_TILE + N_EXPERTS
    n_slots = n_tiles_max * EXPERT_TILE

    def moe(x1, xn2, route, route_t, counts, layer, gains, gain_dtypes, emit_x, split_rows=0):
        dest, tile_expert, n_tiles, pad_start, pad_len, weight_plan = routing_plan(route_t, counts, n_tiles_max)
        xs = dispatch_call(xn2, dest, pad_start, pad_len, n_tiles, n_slots)
        ys = experts_call(xs, tile_expert, n_tiles, weight_plan, w_gate, w_up, w_down, layer)
        return combine_call(x1, route, ys, dest, gains, gain_dtypes, emit_x, split_rows)

    def router_weights(layer):
        pad = LANES - N_GROUPS - N_EXPERTS
        w = jnp.concatenate([w_group[layer], w_router[layer], jnp.zeros((D, pad), F32)], axis=1)
        w_hi = w.astype(BF16)
        w_lo = (w - w_hi.astype(F32)).astype(BF16)
        w = jnp.concatenate([w_hi, w_lo], axis=1)
        b = jnp.concatenate([b_group[layer], b_router[layer], jnp.zeros((pad,), F32)]).reshape(1, LANES)
        return w, b

    x, xn = norm_in_call(x_prompt.reshape(n_p, D), x_sample.reshape(n_s, D), g_mix[0])
    conv_p, conv_s = [], []
    for l in range(n_a):
        st = state_conv[l]
        z = jnp.zeros((Bs, Ss - 1, D), F32)
        inj1 = jnp.concatenate([st[:, 1:2], z], axis=1).reshape(n_s, D)
        inj2 = jnp.concatenate([st, z[:, 1:]], axis=1).reshape(n_s, D)
        g, cp, cs = conv_in_call(xn, w_in_a, conv_a, l, inj1, inj2, n_p, Bs, Ss)
        conv_p.append(cp.reshape(Bp, 2, D))
        conv_s.append(cs)
        wr, br = router_weights(l)
        routed = proj_route_call(g, w_out_a, (l,), x, g_ffn[l], wr, br)
        if l + 1 < n_a:
            x, xn = moe(*routed, l, [g_mix[l + 1]], [BF16], True)
        else:
            x, xn_kv, xn = moe(*routed, l, [g_kv, g_mix[l + 1]], [BF16, BF16], True)

    keep_p = min(N_BACK_CHUNKS * CHUNK, Tp)
    mm_tile = 512
    assert keep_p == mm_tile and n_s == mm_tile
    kv_bf, kv_tail = matmul_call(xn_kv, w_kv, (), [BF16, F32], tail_tiles=(0, 2), tm=mm_tile, name="kv_proj")

    y_prompt = y_sample = None
    for j in range(depth - n_a):
        l = n_a + j
        (q,) = matmul_call(xn, w_q_b, (j,), [BF16], out_scale=head_dim ** -0.5 * LOG2E, name="q_proj")
        bias = band_bias(rel_bias_b[j])
        o = attn_prompt_call(q, kv_bf, bias, n_p, n_heads, head_dim)
        o = attn_step_call(q, cache_k, cache_v, kv_bf, bias[-1, :, :Ss, :n_cache + Ss], o, n_p, n_heads, head_dim)
        wr, br = router_weights(l)
        routed = proj_route_call(o, w_o_b, (j,), x, g_ffn[l], wr, br)
        if l + 1 < depth:
            x, xn = moe(*routed, l, [g_mix[l + 1]], [BF16], True)
        else:
            y_prompt, y_sample = moe(*routed, l, [g_final], [F32], False, split_rows=n_p)

    new_conv_prompt = jnp.stack(conv_p, axis=0)
    new_conv_sample = jnp.stack(conv_s, axis=0)
    new_k_prompt = kv_tail[:keep_p, :D].reshape(Bp, keep_p, n_heads, head_dim)
    new_v_prompt = kv_tail[:keep_p, D:].reshape(Bp, keep_p, n_heads, head_dim)
    new_k_sample = roll_cache_call(cache_k, kv_tail, keep_p, 0)
    new_v_sample = roll_cache_call(cache_v, kv_tail, keep_p, 1)
    return (y_prompt.reshape(Bp, Tp, D), y_sample.reshape(Bs, Ss, D), new_conv_prompt, new_k_prompt, new_v_prompt,
            new_conv_sample, new_k_sample, new_v_sample)
```

```python
import functools
import math

import jax
import jax.numpy as jnp
from jax import lax
from jax.experimental import pallas as pl
from jax.experimental.pallas import tpu as pltpu

F32 = jnp.float32
BF16 = jnp.bfloat16

CHUNK = 64
N_BACK_CHUNKS = 8
REL_CLIP = 128
N_GROUPS = 4
EXPERTS_PER_GROUP = 4
N_EXPERTS = N_GROUPS * EXPERTS_PER_GROUP
EPS = 1e-6

LANES = 128
SUBLANES = 8
EXPERT_TILE = 256
ATTN_GROUP = 4
VMEM_LIMIT = 56 * 1024 * 1024
LOG2E = math.log2(math.e)


def _params(sem, **kw):
    return pltpu.CompilerParams(dimension_semantics=sem, vmem_limit_bytes=VMEM_LIMIT, **kw)


def _rms(x):
    return x * lax.rsqrt(jnp.mean(x * x, axis=-1, keepdims=True) + EPS)


def _norm_in_kernel(xa_ref, xb_ref, g_ref, x_ref, xn_ref, *, a_tiles):
    def emit(src):
        x = src[...]
        x_ref[...] = x
        xn_ref[...] = (_rms(x) * g_ref[...]).astype(xn_ref.dtype)

    @pl.when(pl.program_id(0) < a_tiles)
    def _():
        emit(xa_ref)

    @pl.when(pl.program_id(0) >= a_tiles)
    def _():
        emit(xb_ref)


def norm_in_call(xa, xb, g, tm=512):
    (na, D), nb = xa.shape, xb.shape[0]
    a_tiles = na // tm
    assert na % tm == 0 and nb % tm == 0
    M = na + nb
    return pl.pallas_call(
        functools.partial(_norm_in_kernel, a_tiles=a_tiles),
        grid=(M // tm,),
        in_specs=[pl.BlockSpec((tm, D), lambda i: (jnp.minimum(i, a_tiles - 1), 0)),
                  pl.BlockSpec((tm, D), lambda i: (jnp.maximum(i - a_tiles, 0), 0)),
                  pl.BlockSpec((1, D), lambda i: (0, 0))],
        out_specs=[pl.BlockSpec((tm, D), lambda i: (i, 0)), pl.BlockSpec((tm, D), lambda i: (i, 0))],
        out_shape=[jax.ShapeDtypeStruct((M, D), F32), jax.ShapeDtypeStruct((M, D), BF16)],
        compiler_params=_params(("arbitrary",)),
        name="norm_in",
    )(xa, xb, g.reshape(1, D))


def _conv_in_kernel(xn_ref, wb_ref, wc_ref, wh_ref, cw_ref, inj1_ref, inj2_ref,
                    g_ref, cp_ref, cs_ref, wbs, wcs, whs, carry, u_s, *, n_prompt_tiles, n_seq, seq_len):
    i = pl.program_id(1)
    tm = u_s.shape[0]

    @pl.when(i == 0)
    def _():
        wbs[...] = wb_ref[...].astype(BF16)
        wcs[...] = wc_ref[...].astype(BF16)
        whs[...] = wh_ref[...].astype(BF16)
        carry[...] = jnp.zeros_like(carry)

    xn = xn_ref[...]
    b = jnp.dot(xn, wbs[...], preferred_element_type=F32)
    c = jnp.dot(xn, wcs[...], preferred_element_type=F32)
    h = jnp.dot(xn, whs[...], preferred_element_type=F32)
    u = c * h
    u_s[...] = u
    is_sample = i >= n_prompt_tiles
    row = lax.broadcasted_iota(jnp.int32, u.shape, 0)
    rowm = row & jnp.where(is_sample, seq_len - 1, 0x7FFFFFFF)
    c0 = carry[0:1, :]
    c1 = carry[1:2, :]
    e1 = jnp.where(is_sample, inj1_ref[...], c1)
    e2 = jnp.where(is_sample, inj2_ref[...], jnp.where(row == 0, c0, c1))
    up1 = jnp.where(rowm == 0, e1, pltpu.roll(u, 1, 0))
    up2 = jnp.where(rowm < 2, e2, pltpu.roll(u, 2, 0))
    cw = cw_ref[...]
    conv = cw[0:1, :] * up2 + cw[1:2, :] * up1 + cw[2:3, :] * u
    g_ref[...] = (b * conv).astype(g_ref.dtype)
    carry[...] = u_s[tm - 2:tm, :]

    @pl.when(i == n_prompt_tiles - 1)
    def _():
        cp_ref[...] = u_s[tm - 2:tm, :]

    @pl.when(i == n_prompt_tiles)
    def _():
        for s in range(n_seq):
            cs_ref[s] = u_s[(s + 1) * seq_len - 2:(s + 1) * seq_len, :]


def conv_in_call(xn, w_in_all, conv_all, layer, inj1, inj2, n_prompt_rows, n_seq, seq_len, tn=512):
    M, D = xn.shape
    tm = n_seq * seq_len
    assert M == n_prompt_rows + tm and n_prompt_rows % tm == 0 and D % tn == 0
    nj, ni = D // tn, M // tm
    kern = functools.partial(_conv_in_kernel, n_prompt_tiles=n_prompt_rows // tm, n_seq=n_seq, seq_len=seq_len)
    wspec = lambda k: pl.BlockSpec((None, D, tn), lambda j, i, k=k: (layer, 0, j + k * nj))
    return pl.pallas_call(
        kern,
        grid=(nj, ni),
        in_specs=[
            pl.BlockSpec((tm, D), lambda j, i: (i, 0)),
            wspec(0), wspec(1), wspec(2),
            pl.BlockSpec((None, 3, tn), lambda j, i: (layer, 0, j)),
            pl.BlockSpec((tm, tn), lambda j, i: (0, j)),
            pl.BlockSpec((tm, tn), lambda j, i: (0, j)),
        ],
        out_specs=[
            pl.BlockSpec((tm, tn), lambda j, i: (i, j)),
            pl.BlockSpec((2, tn), lambda j, i: (0, j)),
            pl.BlockSpec((n_seq, 2, tn), lambda j, i: (0, 0, j)),
        ],
        out_shape=[
            jax.ShapeDtypeStruct((M, D), BF16),
            jax.ShapeDtypeStruct((2, D), F32),
            jax.ShapeDtypeStruct((n_seq, 2, D), F32),
        ],
        scratch_shapes=[pltpu.VMEM((D, tn), BF16)] * 3 + [pltpu.VMEM((2, tn), F32), pltpu.VMEM((tm, tn), F32)],
        compiler_params=_params(("arbitrary", "arbitrary")),
        name="conv_in",
    )(xn, w_in_all, w_in_all, w_in_all, conv_all, inj1, inj2)


def _matmul_kernel(x_ref, w_ref, *rest, out_scale):
    out_refs, ws = rest[:-1], rest[-1]

    @pl.when(pl.program_id(1) == 0)
    def _():
        ws[...] = w_ref[...].astype(BF16)

    y = jnp.dot(x_ref[...], ws[...], preferred_element_type=F32)
    if out_scale is not None:
        y = y * out_scale
    for o in out_refs:
        o[...] = y.astype(o.dtype)


def matmul_call(x, w, w_index, out_dtypes, tail_tiles=(), out_scale=None, tm=512, tn=1024, name="matmul"):
    M, K = x.shape
    N = w.shape[-1]
    ni = M // tm
    lead = tuple(w_index)
    wblock = (None,) * len(lead) + (K, tn)
    tails = tuple(tail_tiles) + (0,) * (len(out_dtypes) - len(tail_tiles))
    out_specs, out_shape = [], []
    for dt, n in zip(out_dtypes, tails):
        if n:
            out_specs.append(pl.BlockSpec((tm, tn), lambda j, i, n=n: (jnp.maximum(i - (ni - n), 0), j)))
            out_shape.append(jax.ShapeDtypeStruct((n * tm, N), dt))
        else:
            out_specs.append(pl.BlockSpec((tm, tn), lambda j, i: (i, j)))
            out_shape.append(jax.ShapeDtypeStruct((M, N), dt))
    return pl.pallas_call(
        functools.partial(_matmul_kernel, out_scale=out_scale),
        grid=(N // tn, ni),
        in_specs=[pl.BlockSpec((tm, K), lambda j, i: (i, 0)),
                  pl.BlockSpec(wblock, lambda j, i: lead + (0, j))],
        out_specs=out_specs,
        out_shape=out_shape,
        scratch_shapes=[pltpu.VMEM((K, tn), BF16)],
        compiler_params=_params(("arbitrary", "arbitrary")),
        name=name,
    )(x, w)


def _route(logits, carry):
    tm = logits.shape[0]
    lane = lax.broadcasted_iota(jnp.int32, logits.shape, 1).astype(F32)
    neg = -jnp.inf
    big = float(LANES)
    gl = jnp.where(lane < N_GROUPS, logits, neg)
    gmax = jnp.max(gl, axis=-1, keepdims=True)
    gidx = jnp.min(jnp.where(gl == gmax, lane, big), axis=-1, keepdims=True)
    gp = 1.0 / jnp.sum(jnp.exp(gl - gmax), axis=-1, keepdims=True)
    lo = N_GROUPS + gidx * EXPERTS_PER_GROUP
    el = jnp.where(lane >= lo, jnp.where(lane < lo + EXPERTS_PER_GROUP, logits, neg), neg)
    t1 = jnp.max(el, axis=-1, keepdims=True)
    i1 = jnp.min(jnp.where(el == t1, lane, big), axis=-1, keepdims=True)
    el2 = jnp.where(lane == i1, neg, el)
    t2 = jnp.max(el2, axis=-1, keepdims=True)
    i2 = jnp.min(jnp.where(el2 == t2, lane, big), axis=-1, keepdims=True)
    ex = jnp.exp(t2 - t1)
    den = 1.0 + ex
    w0 = (1.0 / den) * gp
    w1 = (ex / den) * gp
    hit1 = lane == i1
    hit2 = lane == i2
    cnt = jnp.where(hit1, 1.0, jnp.where(hit2, 1.0, 0.0))
    r_i = lax.broadcasted_iota(jnp.int32, (tm, tm), 0)
    c_i = lax.broadcasted_iota(jnp.int32, (tm, tm), 1)
    tri = jnp.where(c_i < r_i, 1.0, 0.0).astype(BF16)
    before = jnp.dot(tri, cnt.astype(BF16), preferred_element_type=F32) + carry[...]
    rank0 = jnp.sum(jnp.where(hit1, before, 0.0), axis=-1, keepdims=True)
    rank1 = jnp.sum(jnp.where(hit2, before, 0.0), axis=-1, keepdims=True)
    carry[...] = carry[...] + jnp.sum(cnt, axis=0, keepdims=True)
    vals = (i1 - N_GROUPS, i2 - N_GROUPS, rank0, rank1, w0, w1)
    route = jnp.zeros_like(logits)
    for k, v in enumerate(vals):
        route = jnp.where(lane == float(k), v, route)
    return route


def _proj_route_kernel(a_ref, w_ref, x_ref, gf_ref, wr_ref, br_ref,
                       x1_ref, xn2_ref, route_ref, routet_ref, cnt_ref, ws, carry):
    K = ws.shape[0]

    @pl.when(pl.program_id(0) == 0)
    def _():
        step = 256

        def body(k, _):
            r = pl.multiple_of(k * step, step)
            ws[pl.ds(r, step), :] = w_ref[pl.ds(r, step), :].astype(BF16)
            return 0

        lax.fori_loop(0, K // step, body, 0)
        carry[...] = jnp.zeros_like(carry)

    y = jnp.dot(a_ref[...], ws[...], preferred_element_type=F32)
    x1 = x_ref[...] + y
    x1_ref[...] = x1
    xn = _rms(x1) * gf_ref[...]
    xn2_ref[...] = xn
    xh = xn.astype(BF16)
    xl = (xn - xh.astype(F32)).astype(BF16)
    p1 = jnp.dot(xh, wr_ref[...], preferred_element_type=F32)
    p2 = jnp.dot(xl, wr_ref[:, :LANES], preferred_element_type=F32)
    logits = p1[:, :LANES] + (p1[:, LANES:] + p2) + br_ref[...]
    route = _route(logits, carry)
    route_ref[...] = route
    routet_ref[...] = route.T[:SUBLANES, :]
    cnt_ref[...] = carry[...]


def proj_route_call(a, w_all, w_index, x, g_ffn, w_route, b_route, tm=256):
    M, K = a.shape
    D = x.shape[1]
    lead = tuple(w_index)
    wblock = (None,) * len(lead) + (K, D)
    const = lambda i: (0, 0)
    rowblk = lambda i: (i, 0)
    return pl.pallas_call(
        _proj_route_kernel,
        grid=(M // tm,),
        in_specs=[
            pl.BlockSpec((tm, K), rowblk),
            pl.BlockSpec(wblock, lambda i: lead + (0, 0), pipeline_mode=pl.Buffered(1)),
            pl.BlockSpec((tm, D), rowblk),
            pl.BlockSpec((1, D), const),
            pl.BlockSpec((D, 2 * LANES), const),
            pl.BlockSpec((1, LANES), const),
        ],
        out_specs=[
            pl.BlockSpec((tm, D), rowblk),
            pl.BlockSpec((tm, D), rowblk),
            pl.BlockSpec((tm, LANES), rowblk),
            pl.BlockSpec((SUBLANES, tm), lambda i: (0, i)),
            pl.BlockSpec((1, LANES), const),
        ],
        out_shape=[
            jax.ShapeDtypeStruct((M, D), F32),
            jax.ShapeDtypeStruct((M, D), F32),
            jax.ShapeDtypeStruct((M, LANES), F32),
            jax.ShapeDtypeStruct((SUBLANES, M), F32),
            jax.ShapeDtypeStruct((1, LANES), F32),
        ],
        scratch_shapes=[pltpu.VMEM((K, D), BF16), pltpu.VMEM((1, LANES), F32)],
        compiler_params=_params(("arbitrary",)),
        name="proj_route",
    )(a, w_all, x, g_ffn.reshape(1, D), w_route, b_route)


def routing_plan(route_t, counts, n_tiles_max):
    te = EXPERT_TILE
    eid = route_t[0:2].astype(jnp.int32)
    rank = route_t[2:4].astype(jnp.int32)
    cnt = counts[0, N_GROUPS:N_GROUPS + N_EXPERTS].astype(jnp.int32)
    tiles = (cnt + te - 1) // te
    tile_end = jnp.cumsum(tiles)
    offs = (tile_end - tiles) * te
    dest = rank
    for e in range(N_EXPERTS):
        dest = dest + jnp.where(eid == e, offs[e], 0)
    n_tiles = tile_end[-1]
    tid = jnp.arange(n_tiles_max, dtype=jnp.int32)
    tile_expert = jnp.sum((jnp.minimum(tid, n_tiles - 1)[:, None] >= tile_end[None, :]).astype(jnp.int32), axis=1)
    pad_start = offs + cnt
    pad_len = tiles * te - cnt
    first = jnp.concatenate([jnp.ones((1,), jnp.int32), (tile_expert[1:] != tile_expert[:-1]).astype(jnp.int32)])
    half = (jnp.cumsum(first) - 1) % 2
    nxt_tile = tile_end[tile_expert]
    nxt = jnp.where(nxt_tile < n_tiles, tile_expert[jnp.minimum(nxt_tile, n_tiles_max - 1)], -1)
    return dest.reshape(-1), tile_expert, n_tiles.reshape(1), pad_start, pad_len, (first, half, nxt)


def _slab_row(ref, r):
    return ref.at[r >> 3, pl.ds(r & (SUBLANES - 1), 1), :]


def _dispatch_kernel(dest_ref, pstart_ref, plen_ref, nt_ref, xn_ref, xs_hbm, zbuf, sem, zsem, tsem, *, tm, n_rows):
    i = pl.program_id(0)
    base = i * tm
    tile_slabs = zbuf.shape[0]
    n_tiles_max = xs_hbm.shape[0] // tile_slabs

    def row_copy(k, u, slot):
        return pltpu.make_async_copy(xn_ref.at[k, pl.ds(u, 1), :], _slab_row(xs_hbm, slot), sem)

    def zero_copy(slot):
        return pltpu.make_async_copy(zbuf.at[0, pl.ds(0, 1), :], _slab_row(xs_hbm, slot), zsem)

    def zero_tile_copy(tile):
        return pltpu.make_async_copy(zbuf, xs_hbm.at[pl.ds(tile * tile_slabs, tile_slabs)], tsem)

    def issue(k, _):
        for u in range(SUBLANES):
            r = k * SUBLANES + u
            row_copy(k, u, dest_ref[base + r]).start(priority=0)
            row_copy(k, u, dest_ref[n_rows + base + r]).start(priority=1)
        return 0

    lax.fori_loop(0, tm // SUBLANES, issue, 0)

    @pl.when(i == 0)
    def _():
        zbuf[...] = jnp.zeros_like(zbuf)
        for e in range(N_EXPERTS):
            def zissue(r, _, e=e):
                zero_copy(pstart_ref[e] + r).start()
                return 0

            lax.fori_loop(0, plen_ref[e], zissue, 0)

        def tissue(t, _):
            zero_tile_copy(t).start()
            return 0

        lax.fori_loop(nt_ref[0], n_tiles_max, tissue, 0)
        for e in range(N_EXPERTS):
            def zwait(r, _):
                zero_copy(0).wait()
                return 0

            lax.fori_loop(0, plen_ref[e], zwait, 0)

        def twait(t, _):
            zero_tile_copy(0).wait()
            return 0

        lax.fori_loop(nt_ref[0], n_tiles_max, twait, 0)

    def wait(k, _):
        for u in range(2 * SUBLANES):
            row_copy(0, 0, 0).wait()
        return 0

    lax.fori_loop(0, tm // SUBLANES, wait, 0)


def dispatch_call(xn, dest, pad_start, pad_len, n_tiles, n_slots, tm=512):
    M, D = xn.shape
    kern = functools.partial(_dispatch_kernel, tm=tm, n_rows=M)
    xs = pl.pallas_call(
        kern,
        grid_spec=pltpu.PrefetchScalarGridSpec(
            num_scalar_prefetch=4,
            grid=(M // tm,),
            in_specs=[pl.BlockSpec((tm // SUBLANES, SUBLANES, D), lambda i, *_: (i, 0, 0))],
            out_specs=pl.BlockSpec(memory_space=pl.ANY),
            scratch_shapes=[pltpu.VMEM((EXPERT_TILE // SUBLANES, SUBLANES, D), xn.dtype)]
            + [pltpu.SemaphoreType.DMA(())] * 3,
        ),
        out_shape=jax.ShapeDtypeStruct((n_slots // SUBLANES, SUBLANES, D), xn.dtype),
        compiler_params=_params(("arbitrary",)),
        name="dispatch",
    )(dest, pad_start, pad_len, n_tiles, xn.reshape(M // SUBLANES, SUBLANES, D))
    return xs.reshape(n_slots, D)


def _experts_kernel(te_ref, nt_ref, first_ref, half_ref, nxt_ref, xs_ref, wg_hbm, wu_hbm, wd_hbm, ys_ref,
                    wgs, wus, wds, sg, su, sd, sems, *, layer):
    t = pl.program_id(0)
    valid = t < nt_ref[0]

    def fetch(e, half):
        return (pltpu.make_async_copy(wg_hbm.at[layer, e], sg.at[half], sems.at[half]),
                pltpu.make_async_copy(wu_hbm.at[layer, e], su.at[half], sems.at[half]),
                pltpu.make_async_copy(wd_hbm.at[layer, e], sd.at[half], sems.at[half]))

    @pl.when(t == 0)
    def _():
        for c in fetch(te_ref[0], 0):
            c.start()

    @pl.when(jnp.logical_and(valid, first_ref[t] == 1))
    def _():
        half = half_ref[t]
        for c in fetch(0, half):
            c.wait()

        @pl.when(nxt_ref[t] >= 0)
        def _():
            for c in fetch(nxt_ref[t], 1 - half):
                c.start()

        wgs[...] = sg[half].astype(BF16)
        wus[...] = su[half].astype(BF16)
        wds[...] = sd[half].astype(BF16)

    @pl.when(valid)
    def _():
        x = xs_ref[...].astype(BF16)
        hg = jnp.dot(x, wgs[...], preferred_element_type=F32)
        hu = jnp.dot(x, wus[...], preferred_element_type=F32)
        h = (hg * (1.0 / (1.0 + jnp.exp(-hg)))) * hu
        ys_ref[...] = jnp.dot(h.astype(BF16), wds[...], preferred_element_type=F32)

    @pl.when(jnp.logical_not(valid))
    def _():
        ys_ref[...] = jnp.zeros_like(ys_ref)


def experts_call(xs, tile_expert, n_tiles, weight_plan, w_gate, w_up, w_down, layer):
    S, D = xs.shape
    Fd = w_gate.shape[-1]
    te = EXPERT_TILE
    anyspec = pl.BlockSpec(memory_space=pl.ANY)
    return pl.pallas_call(
        functools.partial(_experts_kernel, layer=layer),
        grid_spec=pltpu.PrefetchScalarGridSpec(
            num_scalar_prefetch=5,
            grid=(S // te,),
            in_specs=[pl.BlockSpec((te, D), lambda t, te_ref, nt_ref, *_: (jnp.minimum(t, nt_ref[0] - 1), 0)),
                      anyspec, anyspec, anyspec],
            out_specs=pl.BlockSpec((te, D), lambda t, *_: (t, 0)),
            scratch_shapes=[pltpu.VMEM((D, Fd), BF16), pltpu.VMEM((D, Fd), BF16), pltpu.VMEM((Fd, D), BF16),
                            pltpu.VMEM((2, D, Fd), F32), pltpu.VMEM((2, D, Fd), F32), pltpu.VMEM((2, Fd, D), F32),
                            pltpu.SemaphoreType.DMA((2,))],
        ),
        out_shape=jax.ShapeDtypeStruct((S, D), F32),
        compiler_params=_params(("arbitrary",)),
        name="experts",
    )(tile_expert, n_tiles, *weight_plan, xs, w_gate, w_up, w_down)


def _combine_kernel(dest_ref, x1_ref, route_ref, ys_hbm, *rest, tm, n_rows, n_gain, emit_x, split_tiles):
    gain_refs = rest[:n_gain]
    n_out = (1 if emit_x else 0) + (2 if split_tiles else n_gain)
    out_refs = rest[n_gain:n_gain + n_out]
    buf0, buf1, sems = rest[n_gain + n_out:]
    i = pl.program_id(0)
    D = x1_ref.shape[1]
    cur = i % 2

    def row_copy(slot, buf, half, k, u):
        return pltpu.make_async_copy(_slab_row(ys_hbm, slot), buf.at[half, k, pl.ds(u, 1), :], sems.at[half])

    def gather(tile, half):
        base = tile * tm

        def issue(k, _):
            for u in range(SUBLANES):
                r = k * SUBLANES + u
                row_copy(dest_ref[base + r], buf0, half, k, u).start(priority=0)
                row_copy(dest_ref[n_rows + base + r], buf1, half, k, u).start(priority=1)
            return 0

        lax.fori_loop(0, tm // SUBLANES, issue, 0)

    @pl.when(i == 0)
    def _():
        gather(0, 0)

    @pl.when(i + 1 < pl.num_programs(0))
    def _():
        gather(i + 1, 1 - cur)

    def wait(k, _):
        for u in range(SUBLANES):
            row_copy(0, buf0, cur, 0, 0).wait()
            row_copy(0, buf1, cur, 0, 0).wait()
        return 0

    lax.fori_loop(0, tm // SUBLANES, wait, 0)

    route = route_ref[...]
    y0 = buf0[cur].reshape(tm, D)
    y1 = buf1[cur].reshape(tm, D)
    x2 = x1_ref[...] + (route[:, 4:5] * y0 + route[:, 5:6] * y1)
    k = 0
    if emit_x:
        out_refs[0][...] = x2
        k = 1
    if n_gain:
        xh = _rms(x2)
        if split_tiles:
            y = (xh * gain_refs[0][...]).astype(out_refs[k].dtype)

            @pl.when(i < split_tiles)
            def _():
                out_refs[k][...] = y

            @pl.when(i >= split_tiles)
            def _():
                out_refs[k + 1][...] = y
        else:
            for g_ref, o_ref in zip(gain_refs, out_refs[k:]):
                o_ref[...] = (xh * g_ref[...]).astype(o_ref.dtype)


def combine_call(x1, route, ys, dest, gains, gain_dtypes, emit_x, split_rows=0, tm=256):
    M, D = x1.shape
    n_gain = len(gains)
    split_tiles = split_rows // tm
    kern = functools.partial(_combine_kernel, tm=tm, n_rows=M, n_gain=n_gain, emit_x=emit_x, split_tiles=split_tiles)
    rowblk = lambda i, d: (i, 0)
    const = lambda i, d: (0, 0)
    out_specs = [pl.BlockSpec((tm, D), rowblk)] if emit_x else []
    out_shape = [jax.ShapeDtypeStruct((M, D), F32)] if emit_x else []
    if split_tiles:
        assert n_gain == 1 and split_rows % tm == 0
        out_specs += [pl.BlockSpec((tm, D), lambda i, d: (jnp.minimum(i, split_tiles - 1), 0)),
                      pl.BlockSpec((tm, D), lambda i, d: (jnp.maximum(i - split_tiles, 0), 0))]
        out_shape += [jax.ShapeDtypeStruct((split_rows, D), gain_dtypes[0]),
                      jax.ShapeDtypeStruct((M - split_rows, D), gain_dtypes[0])]
    else:
        out_specs += [pl.BlockSpec((tm, D), rowblk) for _ in gain_dtypes]
        out_shape += [jax.ShapeDtypeStruct((M, D), dt) for dt in gain_dtypes]
    slabs = (2, tm // SUBLANES, SUBLANES, D)
    return pl.pallas_call(
        kern,
        grid_spec=pltpu.PrefetchScalarGridSpec(
            num_scalar_prefetch=1,
            grid=(M // tm,),
            in_specs=[pl.BlockSpec((tm, D), rowblk), pl.BlockSpec((tm, LANES), rowblk),
                      pl.BlockSpec(memory_space=pl.ANY)] + [pl.BlockSpec((1, D), const)] * n_gain,
            out_specs=out_specs,
            scratch_shapes=[pltpu.VMEM(slabs, F32), pltpu.VMEM(slabs, F32), pltpu.SemaphoreType.DMA((2,))],
        ),
        out_shape=out_shape,
        compiler_params=_params(("arbitrary",)),
        name="combine",
    )(dest, x1, route, ys.reshape(ys.shape[0] // SUBLANES, SUBLANES, D), *[g.reshape(1, D) for g in gains])


def _softmax_pv(s_parts, v_parts):
    m = s_parts[0].max(axis=-1, keepdims=True)
    for s in s_parts[1:]:
        m = jnp.maximum(m, s.max(axis=-1, keepdims=True))
    l = None
    o = None
    for s, v in zip(s_parts, v_parts):
        e = jnp.exp2(s - m)
        ls = jnp.sum(e, axis=-1, keepdims=True)
        ov = jnp.dot(e.astype(BF16), v, preferred_element_type=F32)
        l = ls if l is None else l + ls
        o = ov if o is None else o + ov
    return o / l


def _qk(qh, kh):
    return lax.dot_general(qh, kh, (((1,), (1,)), ((), ())), preferred_element_type=F32)


def _attn_prompt_kernel(q_ref, *rest, n_parts, n_heads, head_dim, n_blocks):
    k_refs = rest[:n_parts]
    v_refs = rest[n_parts:2 * n_parts]
    bias_ref, o_ref = rest[2 * n_parts:]
    pw = k_refs[0].shape[0]

    @pl.when(pl.program_id(0) < n_blocks)
    def _():
        for h in range(n_heads):
            hs = slice(h * head_dim, (h + 1) * head_dim)
            qh = q_ref[:, hs]
            s_parts = [_qk(qh, k_refs[p][:, hs]) + bias_ref[h, :, p * pw:(p + 1) * pw] for p in range(n_parts)]
            blk = pl.program_id(0)
            s_parts = [jnp.where(blk - (n_parts - 1) + p >= 0, s, -jnp.inf) if p < n_parts - 1 else s
                       for p, s in enumerate(s_parts)]
            o_ref[:, hs] = _softmax_pv(s_parts, [v_refs[p][:, hs] for p in range(n_parts)]).astype(o_ref.dtype)

    @pl.when(pl.program_id(0) >= n_blocks)
    def _():
        o_ref[...] = jnp.zeros_like(o_ref)


def attn_prompt_call(q, kv, bias, n_rows, n_heads, head_dim):
    D = n_heads * head_dim
    tq = ATTN_GROUP * CHUNK
    n_parts = (N_BACK_CHUNKS * CHUNK) // tq + 1
    kern = functools.partial(_attn_prompt_kernel, n_parts=n_parts, n_heads=n_heads, head_dim=head_dim,
                             n_blocks=n_rows // tq)
    kspec = lambda p, col: pl.BlockSpec(
        (tq, D), lambda b, p=p, col=col: (jnp.maximum(b - (n_parts - 1) + p, 0), col))
    return pl.pallas_call(
        kern,
        grid=(q.shape[0] // tq,),
        in_specs=[pl.BlockSpec((tq, D), lambda b: (b, 0))]
        + [kspec(p, 0) for p in range(n_parts)] + [kspec(p, 1) for p in range(n_parts)]
        + [pl.BlockSpec(bias.shape, lambda b: (0, 0, 0), pipeline_mode=pl.Buffered(1))],
        out_specs=pl.BlockSpec((tq, D), lambda b: (b, 0)),
        out_shape=jax.ShapeDtypeStruct((q.shape[0], D), BF16),
        compiler_params=_params(("arbitrary",)),
        name="attn_prompt",
    )(q, *([kv] * (2 * n_parts)), bias)


def _attn_step_kernel(q_ref, ck_ref, cv_ref, kn_ref, vn_ref, bias_ref, o_in_ref, o_ref, *, n_heads, head_dim):
    del o_in_ref
    n_cache = ck_ref.shape[0] // n_heads
    for h in range(n_heads):
        hs = slice(h * head_dim, (h + 1) * head_dim)
        qh = q_ref[:, hs]
        kc = ck_ref[pl.ds(h, n_cache, stride=n_heads), :].astype(BF16)
        vc = cv_ref[pl.ds(h, n_cache, stride=n_heads), :].astype(BF16)
        s_parts = [_qk(qh, kc) + bias_ref[h, :, :n_cache], _qk(qh, kn_ref[:, hs]) + bias_ref[h, :, n_cache:]]
        o_ref[:, hs] = _softmax_pv(s_parts, [vc, vn_ref[:, hs]]).astype(o_ref.dtype)


def attn_step_call(q, cache_k, cache_v, kv, bias, o_full, row0, n_heads, head_dim):
    B, n_cache = cache_k.shape[:2]
    D = n_heads * head_dim
    S = CHUNK
    blk0 = row0 // S
    ck = cache_k.reshape(B, n_cache * n_heads, head_dim)
    cv = cache_v.reshape(B, n_cache * n_heads, head_dim)
    kern = functools.partial(_attn_step_kernel, n_heads=n_heads, head_dim=head_dim)
    newspec = lambda col: pl.BlockSpec((S, D), lambda b, col=col: (blk0 + b, col))
    return pl.pallas_call(
        kern,
        grid=(B,),
        in_specs=[newspec(0),
                  pl.BlockSpec((None, n_cache * n_heads, head_dim), lambda b: (b, 0, 0)),
                  pl.BlockSpec((None, n_cache * n_heads, head_dim), lambda b: (b, 0, 0)),
                  newspec(0), newspec(1),
                  pl.BlockSpec(bias.shape, lambda b: (0, 0, 0)),
                  pl.BlockSpec(memory_space=pl.ANY)],
        out_specs=newspec(0),
        out_shape=jax.ShapeDtypeStruct(o_full.shape, o_full.dtype),
        input_output_aliases={6: 0},
        compiler_params=_params(("arbitrary",)),
        name="attn_step",
    )(q, ck, cv, kv, kv, bias, o_full)


def band_bias(table):
    H = table.shape[0]
    tq = ATTN_GROUP * CHUNK
    nk = tq + N_BACK_CHUNKS * CHUNK
    P = tq + nk
    d = jnp.concatenate([-jnp.arange(nk), jnp.zeros((1,), jnp.int32), jnp.arange(tq - 1, 0, -1)])
    w = table[:, jnp.clip(N_BACK_CHUNKS * CHUNK + d, -REL_CLIP, REL_CLIP) + REL_CLIP].astype(F32) * LOG2E
    b = jnp.tile(w, (1, tq))[:, :tq * (P - 1)].reshape(H, tq, P - 1)[:, :, :nk]
    i = jnp.arange(tq)[:, None]
    j = jnp.arange(nk)[None, :]
    qc, jc = i // CHUNK, j // CHUNK
    band = (jc >= qc) & (jc <= qc + N_BACK_CHUNKS)
    return jnp.where(band[None], b, -jnp.inf)


def _roll_cache_kernel(c_ref, new_ref, o_ref, *, n_heads, head_dim, n_new):
    keep = c_ref.shape[0] - n_new * n_heads
    o_ref[0:keep, :] = c_ref[n_new * n_heads:, :]
    new = new_ref[...]
    for h in range(n_heads):
        o_ref[pl.ds(keep + h, n_new, stride=n_heads), :] = new[:, h * head_dim:(h + 1) * head_dim]


def roll_cache_call(cache, new_rows, row0, col):
    B, n_cache, n_heads, head_dim = cache.shape
    D = n_heads * head_dim
    S = CHUNK
    blk0 = row0 // S
    c = cache.reshape(B, n_cache * n_heads, head_dim)
    out = pl.pallas_call(
        functools.partial(_roll_cache_kernel, n_heads=n_heads, head_dim=head_dim, n_new=S),
        grid=(B,),
        in_specs=[pl.BlockSpec((None, n_cache * n_heads, head_dim), lambda b: (b, 0, 0)),
                  pl.BlockSpec((S, D), lambda b: (blk0 + b, col))],
        out_specs=pl.BlockSpec((None, n_cache * n_heads, head_dim), lambda b: (b, 0, 0)),
        out_shape=jax.ShapeDtypeStruct(c.shape, cache.dtype),
        compiler_params=_params(("arbitrary",)),
        name="roll_cache",
    )(c, new_rows)
    return out.reshape(cache.shape)


def kernel(x_prompt, x_sample, state_conv, cache_k, cache_v, g_mix, g_ffn, w_in_a, conv_a, w_out_a, g_kv, w_kv,
           w_q_b, w_o_b, rel_bias_b, w_group, b_group, w_router, b_router, w_gate, w_up, w_down, g_final):
    Bp, Tp, D = x_prompt.shape
    Bs, Ss, _ = x_sample.shape
    n_a = w_in_a.shape[0]
    depth = g_mix.shape[0]
    n_heads, head_dim = cache_k.shape[2], cache_k.shape[3]
    n_cache = cache_k.shape[1]
    assert Bp == 1 and Ss == CHUNK
    n_p = Bp * Tp
    n_s = Bs * Ss
    M = n_p + n_s
    n_tiles_max = (2 * M) // EXPERT_TILE + N_EXPERTS
    n_slots = n_tiles_max * EXPERT_TILE

    def moe(x1, xn2, route, route_t, counts, layer, gains, gain_dtypes, emit_x, split_rows=0):
        dest, tile_expert, n_tiles, pad_start, pad_len, weight_plan = routing_plan(route_t, counts, n_tiles_max)
        xs = dispatch_call(xn2, dest, pad_start, pad_len, n_tiles, n_slots)
        ys = experts_call(xs, tile_expert, n_tiles, weight_plan, w_gate, w_up, w_down, layer)
        return combine_call(x1, route, ys, dest, gains, gain_dtypes, emit_x, split_rows)

    def router_weights(layer):
        pad = LANES - N_GROUPS - N_EXPERTS
        w = jnp.concatenate([w_group[layer], w_router[layer], jnp.zeros((D, pad), F32)], axis=1)
        w_hi = w.astype(BF16)
        w_lo = (w - w_hi.astype(F32)).astype(BF16)
        w = jnp.concatenate([w_hi, w_lo], axis=1)
        b = jnp.concatenate([b_group[layer], b_router[layer], jnp.zeros((pad,), F32)]).reshape(1, LANES)
        return w, b

    x, xn = norm_in_call(x_prompt.reshape(n_p, D), x_sample.reshape(n_s, D), g_mix[0])
    conv_p, conv_s = [], []
    for l in range(n_a):
        st = state_conv[l]
        z = jnp.zeros((Bs, Ss - 1, D), F32)
        inj1 = jnp.concatenate([st[:, 1:2], z], axis=1).reshape(n_s, D)
        inj2 = jnp.concatenate([st, z[:, 1:]], axis=1).reshape(n_s, D)
        g, cp, cs = conv_in_call(xn, w_in_a, conv_a, l, inj1, inj2, n_p, Bs, Ss)
        conv_p.append(cp.reshape(Bp, 2, D))
        conv_s.append(cs)
        wr, br = router_weights(l)
        routed = proj_route_call(g, w_out_a, (l,), x, g_ffn[l], wr, br)
        if l + 1 < n_a:
            x, xn = moe(*routed, l, [g_mix[l + 1]], [BF16], True)
        else:
            x, xn_kv, xn = moe(*routed, l, [g_kv, g_mix[l + 1]], [BF16, BF16], True)

    keep_p = min(N_BACK_CHUNKS * CHUNK, Tp)
    mm_tile = 512
    assert keep_p == mm_tile and n_s == mm_tile
    kv_bf, kv_tail = matmul_call(xn_kv, w_kv, (), [BF16, F32], tail_tiles=(0, 2), tm=mm_tile, name="kv_proj")

    y_prompt = y_sample = None
    for j in range(depth - n_a):
        l = n_a + j
        (q,) = matmul_call(xn, w_q_b, (j,), [BF16], out_scale=head_dim ** -0.5 * LOG2E, name="q_proj")
        bias = band_bias(rel_bias_b[j])
        o = attn_prompt_call(q, kv_bf, bias, n_p, n_heads, head_dim)
        o = attn_step_call(q, cache_k, cache_v, kv_bf, bias[:, :Ss, :n_cache + Ss], o, n_p, n_heads, head_dim)
        wr, br = router_weights(l)
        routed = proj_route_call(o, w_o_b, (j,), x, g_ffn[l], wr, br)
        if l + 1 < depth:
            x, xn = moe(*routed, l, [g_mix[l + 1]], [BF16], True)
        else:
            y_prompt, y_sample = moe(*routed, l, [g_final], [F32], False, split_rows=n_p)

    new_conv_prompt = jnp.stack(conv_p, axis=0)
    new_conv_sample = jnp.stack(conv_s, axis=0)
    new_k_prompt = kv_tail[:keep_p, :D].reshape(Bp, keep_p, n_heads, head_dim)
    new_v_prompt = kv_tail[:keep_p, D:].reshape(Bp, keep_p, n_heads, head_dim)
    new_k_sample = roll_cache_call(cache_k, kv_tail, keep_p, 0)
    new_v_sample = roll_cache_call(cache_v, kv_tail, keep_p, 1)
    return (y_prompt.reshape(Bp, Tp, D), y_sample.reshape(Bs, Ss, D), new_conv_prompt, new_k_prompt, new_v_prompt,
            new_conv_sample, new_k_sample, new_v_sample)
```
